```python
import math
import jax, jax.numpy as jnp
from jax import lax
import numpy as np

D_MODEL = 1024
BATCH = 1
SEQ = 16384
DEPTH = 2
DEC_BATCH = 8
DEC_SEQ = 4096
PAST_LEN = 128

HEAD_DIM = 64
BLOCK = 128
EPS = 1e-6
SUBLN_EPS = 1e-5
NEG = -1e30
A_HEADS = 4
A_KV_HEADS = 2
A_WINDOW = 128
B_HEADS = 4
B_VDIM = 2 * HEAD_DIM
C_PAIRS = ((128, 1), (512, 4), (2048, 16))
C_GROUPS = 3
C_HEADS = 4
C_SIDE = 64
C_REACH = 1024
D_FF = 2816
CONV_W = 3
A_Q = A_HEADS * HEAD_DIM
A_KV = A_KV_HEADS * HEAD_DIM
B_QK = B_HEADS * 2 * HEAD_DIM
B_V = B_HEADS * B_VDIM
C_QKV = C_GROUPS * C_HEADS * HEAD_DIM
D_IN = A_Q + 2 * A_KV + 2 * B_QK + B_V + 3 * C_QKV
D_MIX = A_Q + B_V + C_HEADS * HEAD_DIM

kernel_name = "hymba_style_hybrid_encoder"


def rmsnorm(x, g, eps=EPS):
    xf = x.astype(jnp.float32)
    y = xf * lax.rsqrt(jnp.mean(xf * xf, axis=-1, keepdims=True) + eps)
    return (y * g.astype(jnp.float32)).astype(x.dtype)


def alibi_slopes(n):
    return jnp.asarray(2.0 ** (-8.0 * np.arange(1, n + 1) / n), dtype=jnp.float32)


def c_offsets():
    return jnp.asarray(np.stack([d * np.arange(-(w // (2 * d)), w // (2 * d) + 1) for (w, d) in C_PAIRS]), dtype=jnp.int32)


def window_gqa_sink(q, k, v, sink):
    B, S = q.shape[0], q.shape[1]
    nb = S // BLOCK
    g = A_HEADS // A_KV_HEADS
    scale = HEAD_DIM ** -0.5
    qb = q.astype(jnp.float32).reshape(B, nb, BLOCK, A_KV_HEADS, g, HEAD_DIM)
    pad = ((0, 0), (BLOCK, BLOCK), (0, 0), (0, 0))

    def band(t):
        tb = jnp.pad(t.astype(jnp.float32), pad).reshape(B, nb + 2, BLOCK, A_KV_HEADS, HEAD_DIM)
        return jnp.concatenate([tb[:, :-2], tb[:, 1:-1], tb[:, 2:]], axis=2)

    kb, vb = band(k), band(v)
    qi = jnp.arange(BLOCK)[:, None]
    kc = jnp.arange(3 * BLOCK)[None, :]
    dist = kc - BLOCK - qi
    s_pos = jnp.arange(nb)[:, None, None] * BLOCK + kc[None] - BLOCK
    valid = (jnp.abs(dist)[None] <= A_WINDOW) & (s_pos >= 0) & (s_pos < S)
    slopes = alibi_slopes(A_HEADS).reshape(A_KV_HEADS, g)
    logits = jnp.einsum('bnqhgd,bnkhd->bnhgqk', qb, kb) * scale
    logits = logits - slopes[:, :, None, None] * jnp.abs(dist).astype(jnp.float32)
    logits = jnp.where(valid[None, :, None, None], logits, NEG)
    sink_b = sink.astype(jnp.float32).reshape(A_KV_HEADS, g)[:, :, None, None]
    m = jnp.maximum(jnp.max(logits, axis=-1, keepdims=True), sink_b)
    p = jnp.exp(logits - m)
    den = jnp.sum(p, axis=-1, keepdims=True) + jnp.exp(sink_b - m)
    o = jnp.einsum('bnhgqk,bnkhd->bnqhgd', p / den, vb)
    return o.reshape(B, S, A_HEADS * HEAD_DIM).astype(q.dtype)


def diff_attention(q, k, v, lam, lam_init, subln_g):
    B, S = q.shape[0], q.shape[1]
    nb = S // BLOCK
    scale = HEAD_DIM ** -0.5
    slopes = alibi_slopes(B_HEADS)
    kf = k.astype(jnp.float32)
    vf = v.astype(jnp.float32)
    qb = jnp.moveaxis(q.astype(jnp.float32).reshape(B, nb, BLOCK, B_HEADS, 2, HEAD_DIM), 1, 0)
    kpos = jnp.arange(S)

    def one_block(args):
        qblk, i = args
        qpos = i * BLOCK + jnp.arange(BLOCK)
        logits = jnp.einsum('bqhmd,bkhmd->bhmqk', qblk, kf) * scale
        logits = logits - slopes[:, None, None, None] * jnp.abs(qpos[:, None] - kpos[None, :]).astype(jnp.float32)
        a = jax.nn.softmax(logits, axis=-1)
        w = a[:, :, 0] - lam * a[:, :, 1]
        return jnp.einsum('bhqk,bkhe->bqhe', w, vf)

    o = lax.map(one_block, (qb, jnp.arange(nb)))
    o = jnp.moveaxis(o, 0, 1).reshape(B, S, B_HEADS, B_VDIM)
    o = rmsnorm(o, subln_g, SUBLN_EPS) * (1.0 - lam_init)
    return o.reshape(B, S, B_HEADS * B_VDIM).astype(q.dtype)


def dilated_attention(q, k, v):
    B, S = q.shape[0], q.shape[1]
    nb = S // BLOCK
    scale = HEAD_DIM ** -0.5
    span = BLOCK + 2 * C_REACH
    offs = c_offsets()
    slopes = alibi_slopes(C_GROUPS * C_HEADS).reshape(C_GROUPS, C_HEADS)
    pad = ((0, 0), (C_REACH, C_REACH), (0, 0), (0, 0), (0, 0))
    kp = jnp.pad(k.astype(jnp.float32), pad)
    vp = jnp.pad(v.astype(jnp.float32), pad)
    qb = jnp.moveaxis(q.astype(jnp.float32).reshape(B, nb, BLOCK, C_GROUPS, C_HEADS, HEAD_DIM), 1, 0)
    loc = jnp.arange(BLOCK)[:, None, None] + C_REACH + offs[None]
    gidx = jnp.arange(C_GROUPS)[None, :, None]
    dpen = jnp.abs(offs).astype(jnp.float32)[:, None, :]

    def one_block(args):
        qblk, i = args
        ks = lax.dynamic_slice_in_dim(kp, i * BLOCK, span, axis=1)
        vs = lax.dynamic_slice_in_dim(vp, i * BLOCK, span, axis=1)
        kg = ks[:, loc, gidx]
        vg = vs[:, loc, gidx]
        spos = i * BLOCK + jnp.arange(BLOCK)[:, None, None] + offs[None]
        valid = (spos >= 0) & (spos < S)
        logits = jnp.einsum('bqghd,bqgjhd->bqghj', qblk, kg) * scale
        logits = logits - slopes[:, :, None] * dpen
        logits = jnp.where(valid[None, :, :, None, :], logits, NEG)
        lse = jax.nn.logsumexp(logits, axis=-1)
        og = jnp.einsum('bqghj,bqgjhd->bqghd', jnp.exp(logits - lse[..., None]), vg)
        wg = jax.nn.softmax(lse, axis=2)
        return jnp.einsum('bqgh,bqghd->bqhd', wg, og)

    o = lax.map(one_block, (qb, jnp.arange(nb)))
    return jnp.moveaxis(o, 0, 1).reshape(B, S, C_HEADS * HEAD_DIM).astype(q.dtype)


def conv_gated_mlp(h, w_up, conv_w, conv_b, w_down):
    u = h @ w_up
    up = jnp.pad(u, ((0, 0), (1, 1), (0, 0)))
    u = up[:, :-2] * conv_w[0] + up[:, 1:-1] * conv_w[1] + up[:, 2:] * conv_w[2] + conv_b
    gate, val = jnp.split(u, 2, axis=-1)
    return (jax.nn.silu(gate) * val) @ w_down


def trunk(x, ln1, w_in, a_sink, lam_q1, lam_k1, lam_q2, lam_k2, subln, w_out,
          ln2, w_up, conv_w, conv_b, w_down, ln_f):
    B, S = x.shape[0], x.shape[1]
    cuts = np.cumsum([A_Q, A_KV, A_KV, B_QK, B_QK, B_V, C_QKV, C_QKV])
    for l in range(DEPTH):
        h = rmsnorm(x, ln1[l])
        proj = h @ w_in[l]
        aq, ak, av, bq, bk, bv, cq, ck, cv = jnp.split(proj, [int(c) for c in cuts], axis=-1)
        o_a = window_gqa_sink(aq.reshape(B, S, A_HEADS, HEAD_DIM),
                              ak.reshape(B, S, A_KV_HEADS, HEAD_DIM),
                              av.reshape(B, S, A_KV_HEADS, HEAD_DIM), a_sink[l])
        lam_init = 0.8 - 0.6 * math.exp(-0.3 * l)
        lam = (jnp.exp(jnp.sum(lam_q1[l].astype(jnp.float32) * lam_k1[l].astype(jnp.float32)))
               - jnp.exp(jnp.sum(lam_q2[l].astype(jnp.float32) * lam_k2[l].astype(jnp.float32))) + lam_init)
        o_b = diff_attention(bq.reshape(B, S, B_HEADS, 2, HEAD_DIM),
                             bk.reshape(B, S, B_HEADS, 2, HEAD_DIM),
                             bv.reshape(B, S, B_HEADS, B_VDIM), lam, lam_init, subln[l])
        cshape = (B, S, C_GROUPS, C_HEADS, HEAD_DIM)
        o_c = dilated_attention(cq.reshape(cshape), ck.reshape(cshape), cv.reshape(cshape))
        x = x + jnp.concatenate([o_a, o_b, o_c], axis=-1) @ w_out[l]
        x = x + conv_gated_mlp(rmsnorm(x, ln2[l]), w_up[l], conv_w[l], conv_b[l], w_down[l])
    return rmsnorm(x, ln_f)


def setup_inputs(seed: int = 0) -> dict:
    key = jax.random.key(seed)
    ks = jax.random.split(key, 20)
    f32 = jnp.float32
    nrm = lambda k, s, sc: jax.random.normal(k, s, f32) * sc
    return {
        "x_prompt": nrm(ks[0], (BATCH, SEQ, D_MODEL), 1.0),
        "x_sample": nrm(ks[1], (DEC_BATCH, DEC_SEQ, D_MODEL), 1.0),
        "ln1": 1.0 + nrm(ks[2], (DEPTH, D_MODEL), 0.02),
        "w_in": nrm(ks[3], (DEPTH, D_MODEL, D_IN), D_MODEL ** -0.5),
        "a_sink": nrm(ks[4], (DEPTH, A_HEADS), 0.5),
        "lam_q1": nrm(ks[5], (DEPTH, HEAD_DIM), 0.1),
        "lam_k1": nrm(ks[6], (DEPTH, HEAD_DIM), 0.1),
        "lam_q2": nrm(ks[7], (DEPTH, HEAD_DIM), 0.1),
        "lam_k2": nrm(ks[8], (DEPTH, HEAD_DIM), 0.1),
        "subln": 1.0 + nrm(ks[9], (DEPTH, B_VDIM), 0.02),
        "w_out": nrm(ks[10], (DEPTH, D_MIX, D_MODEL), 0.5 * D_MIX ** -0.5),
        "ln2": 1.0 + nrm(ks[11], (DEPTH, D_MODEL), 0.02),
        "w_up": nrm(ks[12], (DEPTH, D_MODEL, 2 * D_FF), D_MODEL ** -0.5),
        "conv_w": nrm(ks[13], (DEPTH, CONV_W, 2 * D_FF), CONV_W ** -0.5),
        "conv_b": nrm(ks[14], (DEPTH, 2 * D_FF), 0.02),
        "w_down": nrm(ks[15], (DEPTH, D_FF, D_MODEL), 0.5 * D_FF ** -0.5),
        "ln_f": 1.0 + nrm(ks[16], (D_MODEL,), 0.02),
    }


def reference(x_prompt, x_sample, ln1, w_in, a_sink, lam_q1, lam_k1, lam_q2, lam_k2, subln,
              w_out, ln2, w_up, conv_w, conv_b, w_down, ln_f):
    y_prompt = trunk(x_prompt, ln1, w_in, a_sink, lam_q1, lam_k1, lam_q2, lam_k2, subln, w_out,
                     ln2, w_up, conv_w, conv_b, w_down, ln_f)
    y_sample = trunk(x_sample, ln1, w_in, a_sink, lam_q1, lam_k1, lam_q2, lam_k2, subln, w_out,
                     ln2, w_up, conv_w, conv_b, w_down, ln_f)
    return (y_prompt, y_sample)
```

```python
import functools
import math

import numpy as np
import jax
import jax.numpy as jnp
from jax import lax
from jax.experimental import pallas as pl
from jax.experimental.pallas import tpu as pltpu

F32 = jnp.float32
BF16 = jnp.bfloat16

D_MODEL = 1024
DEPTH = 2
HEAD_DIM = 64
BLOCK = 128
EPS = 1e-6
SUBLN_EPS = 1e-5
NEG = -1e30
SCALE = HEAD_DIM ** -0.5
A_HEADS = 4
A_KV_HEADS = 2
A_WINDOW = 128
B_HEADS = 4
B_VDIM = 2 * HEAD_DIM
C_PAIRS = ((128, 1), (512, 4), (2048, 16))
C_GROUPS = 3
C_HEADS = 4
D_FF = 2816
A_Q = A_HEADS * HEAD_DIM
A_KV = A_KV_HEADS * HEAD_DIM
B_QK = B_HEADS * 2 * HEAD_DIM
B_V = B_HEADS * B_VDIM
C_QKV = C_GROUPS * C_HEADS * HEAD_DIM
HEADS_W = 4 * HEAD_DIM

VMEM_LIMIT_BYTES = 56 * 1024 * 1024
LANES = 128
SUBLANES = 8


def _alibi_slopes(n):
    return [2.0 ** (-8.0 * k / n) for k in range(1, n + 1)]


def _dot(a, b):
    return jnp.dot(a, b, preferred_element_type=F32)


def _dot_nt(a, b):
    return lax.dot_general(a, b, (((1,), (1,)), ((), ())), preferred_element_type=F32)


def _rms(x, g, eps):
    return x * lax.rsqrt(jnp.mean(x * x, axis=-1, keepdims=True) + eps) * g


_W_AQ, _W_AK, _W_AV = 0, 256, 512
_W_BQ, _W_BK, _W_BV = 768, 1280, 1792
_W_CQ, _W_CK, _W_CV = 2304, 3072, 3840
_W_COLS = 4608


def _inproj_kernel(x_ref, g_ref, w_ref, aq_ref, ak_ref, av_ref, bq_ref, bk_ref, bv_ref,
                   cq_ref, ck_ref, cv_ref, *, seq_len, tm):
    h = _rms(x_ref[...], g_ref[...], EPS).astype(BF16)

    def proj(c0, c1):
        return _dot(h, w_ref[:, c0:c1])

    aq_ref[...] = (proj(_W_AQ, _W_AK) * SCALE).astype(BF16)
    ak_ref[...] = proj(_W_AK, _W_AV).astype(BF16)
    av_ref[...] = proj(_W_AV, _W_BQ).astype(BF16)
    bq_ref[...] = (proj(_W_BQ, _W_BK) * SCALE).astype(BF16)
    pos = (pl.program_id(0) * tm + lax.broadcasted_iota(jnp.int32, (tm, LANES), 0)) % seq_len
    lane = lax.broadcasted_iota(jnp.int32, (tm, LANES), 1)
    aug = jnp.where(lane < 2, 1.0,
                    jnp.where(lane == 2, (pos >> 7).astype(F32),
                              jnp.where(lane == 3, (pos & 127).astype(F32), 0.0))).astype(BF16)
    ones = jnp.ones((tm, LANES), BF16)
    bk = proj(_W_BK, _W_BV).astype(BF16)
    bv = proj(_W_BV, _W_CQ).astype(BF16)
    for hh in range(B_HEADS):
        bk_ref[hh, :, 0:LANES] = bk[:, hh * LANES:(hh + 1) * LANES]
        bk_ref[hh, :, LANES:2 * LANES] = aug
        bv_ref[hh, :, 0:LANES] = bv[:, hh * LANES:(hh + 1) * LANES]
        bv_ref[hh, :, LANES:2 * LANES] = ones
    cq_ref[...] = (proj(_W_CQ, _W_CK) * SCALE).astype(BF16)
    ck_ref[...] = proj(_W_CK, _W_CV).astype(BF16)
    cv_ref[...] = proj(_W_CV, _W_COLS).astype(BF16)


def _in_proj(x, g, w, seq_len):
    T = x.shape[0]
    tm = min(512, T)
    row = lambda i: (i, 0)
    const = lambda i: (0, 0)
    outs = [
        (jax.ShapeDtypeStruct((T, 256), BF16), pl.BlockSpec((tm, 256), row)),
        (jax.ShapeDtypeStruct((T, 256), BF16), pl.BlockSpec((tm, 256), row)),
        (jax.ShapeDtypeStruct((T, 256), BF16), pl.BlockSpec((tm, 256), row)),
        (jax.ShapeDtypeStruct((T, B_QK), BF16), pl.BlockSpec((tm, B_QK), row)),
        (jax.ShapeDtypeStruct((B_HEADS, T, 256), BF16), pl.BlockSpec((B_HEADS, tm, 256), lambda i: (0, i, 0))),
        (jax.ShapeDtypeStruct((B_HEADS, T, 256), BF16), pl.BlockSpec((B_HEADS, tm, 256), lambda i: (0, i, 0))),
        (jax.ShapeDtypeStruct((T, C_QKV), BF16), pl.BlockSpec((tm, C_QKV), row)),
        (jax.ShapeDtypeStruct((T, C_QKV), BF16), pl.BlockSpec((tm, C_QKV), row)),
        (jax.ShapeDtypeStruct((T, C_QKV), BF16), pl.BlockSpec((tm, C_QKV), row)),
    ]
    return pl.pallas_call(
        functools.partial(_inproj_kernel, seq_len=seq_len, tm=tm),
        grid=(T // tm,),
        in_specs=[pl.BlockSpec((tm, D_MODEL), row),
                  pl.BlockSpec((1, D_MODEL), const),
                  pl.BlockSpec((D_MODEL, _W_COLS), const)],
        out_specs=[o[1] for o in outs],
        out_shape=[o[0] for o in outs],
        compiler_params=pltpu.CompilerParams(dimension_semantics=("parallel",),
                                             vmem_limit_bytes=VMEM_LIMIT_BYTES),
    )(x, g.reshape(1, D_MODEL), w)


def _prep_w_in(w):
    g = A_HEADS // A_KV_HEADS
    aq = w[:, :A_Q]
    ak = jnp.repeat(w[:, A_Q:A_Q + A_KV].reshape(D_MODEL, A_KV_HEADS, 1, HEAD_DIM), g, axis=2).reshape(D_MODEL, A_Q)
    av = jnp.repeat(w[:, A_Q + A_KV:A_Q + 2 * A_KV].reshape(D_MODEL, A_KV_HEADS, 1, HEAD_DIM), g, axis=2).reshape(D_MODEL, A_Q)
    return jnp.concatenate([aq, ak, av, w[:, A_Q + 2 * A_KV:]], axis=1).astype(BF16)


def _banded_kernel(*refs, nb, has_sink, emit_lse):
    q_ref, kp_ref, kc_ref, kn_ref, vp_ref, vc_ref, vn_ref, bias_ref = refs[:8]
    rest = refs[8:]
    if has_sink:
        sink_ref, rest = rest[0], rest[1:]
    o_ref = rest[0]
    lse_ref = rest[1] if emit_lse else None

    ib = pl.program_id(2)
    q = q_ref[...].astype(F32)
    kb = jnp.concatenate([kp_ref[...], kc_ref[...], kn_ref[...]], axis=0)
    vb = jnp.concatenate([vp_ref[...], vc_ref[...], vn_ref[...]], axis=0)
    col = lax.broadcasted_iota(jnp.int32, (1, 3 * BLOCK), 1)
    off_seq = ((ib == 0) & (col < BLOCK)) | ((ib == nb - 1) & (col >= 2 * BLOCK))
    edge = jnp.where(off_seq, NEG, 0.0)
    lane_head = lax.broadcasted_iota(jnp.int32, (BLOCK, HEADS_W), 1) // HEAD_DIM
    o_acc = jnp.zeros((BLOCK, HEADS_W), F32)
    lse_acc = jnp.zeros((BLOCK, HEADS_W), F32)
    for h in range(4):
        mine = lane_head == h
        qh = jnp.where(mine, q, 0.0).astype(BF16)
        s = _dot_nt(qh, kb) + bias_ref[h] + edge
        m = jnp.max(s, axis=-1, keepdims=True)
        if has_sink:
            sk = sink_ref[h]
            m = jnp.maximum(m, sk)
        p = jnp.exp(s - m)
        l = jnp.sum(p, axis=-1, keepdims=True)
        if has_sink:
            l = l + jnp.exp(sk - m)
        pv = _dot(p.astype(BF16), vb)
        o_acc = jnp.where(mine, pv / l, o_acc)
        if emit_lse:
            lse_acc = jnp.where(mine, m + jnp.log(l), lse_acc)
    o_ref[...] = o_acc.astype(BF16)
    if emit_lse:
        lse_ref[...] = lse_acc


def _band_bias(window, slopes_eff):
    il = np.arange(BLOCK)[:, None]
    c = np.arange(3 * BLOCK)[None, :]
    dist = np.abs(c - BLOCK - il).astype(np.float64)
    tabs = [np.where(dist <= window, -s * dist, NEG) for s in slopes_eff]
    return jnp.asarray(np.stack(tabs), dtype=F32)


def _banded_attention(q, k, v, *, B, S, d, q_cb, k_cb, v_cb, window, slopes_eff, sink=None, emit_lse=False):
    Sd = S // d
    nb = Sd // BLOCK
    nq, nk, nv = q.shape[1] // HEADS_W, k.shape[1] // HEADS_W, v.shape[1] // HEADS_W
    qv = q.reshape(B, Sd, d * q.shape[1])
    kv = k.reshape(B, Sd, d * k.shape[1])
    vv = v.reshape(B, Sd, d * v.shape[1])
    blk = (None, BLOCK, HEADS_W)

    def spec(ncb, cb, shift):
        def imap(b, r, i):
            return (b, jnp.clip(i + shift, 0, nb - 1), r * ncb + cb)
        return pl.BlockSpec(blk, imap)

    in_specs = [spec(nq, q_cb, 0),
                spec(nk, k_cb, -1), spec(nk, k_cb, 0), spec(nk, k_cb, 1),
                spec(nv, v_cb, -1), spec(nv, v_cb, 0), spec(nv, v_cb, 1),
                pl.BlockSpec((4, BLOCK, 3 * BLOCK), lambda b, r, i: (0, 0, 0))]
    args = [qv, kv, kv, kv, vv, vv, vv, _band_bias(window, slopes_eff)]
    if sink is not None:
        in_specs.append(pl.BlockSpec(memory_space=pltpu.SMEM))
        args.append(sink.astype(F32))
    out_spec = pl.BlockSpec(blk, lambda b, r, i: (b, i, r))
    out_shape = [jax.ShapeDtypeStruct((B, Sd, d * HEADS_W), BF16)]
    out_specs = [out_spec]
    if emit_lse:
        out_shape.append(jax.ShapeDtypeStruct((B, Sd, d * HEADS_W), F32))
        out_specs.append(out_spec)
    res = pl.pallas_call(
        functools.partial(_banded_kernel, nb=nb, has_sink=sink is not None, emit_lse=emit_lse),
        grid=(B, d, nb),
        in_specs=in_specs,
        out_specs=out_specs,
        out_shape=out_shape,
        compiler_params=pltpu.CompilerParams(dimension_semantics=("parallel", "parallel", "parallel"),
                                             vmem_limit_bytes=VMEM_LIMIT_BYTES),
    )(*args)
    return [r.reshape(B * S, HEADS_W) for r in res]


def _diff_kernel(slope_ref, lam_ref, subln_ref, q_ref, k_ref, v_ref, o_ref, qs_ref, m_ref, acc_ref,
                 *, tq, nk, lam_init):
    qi = pl.program_id(2)
    ki = pl.program_id(3)

    @pl.when(ki == 0)
    def _init():
        q = q_ref[...].astype(F32)
        lane = lax.broadcasted_iota(jnp.int32, (tq, LANES), 1)
        pos = qi * tq + lax.broadcasted_iota(jnp.int32, (tq, LANES), 0)
        slope = slope_ref[...]
        base = jnp.where(lane == 0, -slope * 128.0 * (pos >> 7).astype(F32),
                         jnp.where(lane == 1, -slope * (pos & 127).astype(F32),
                                   jnp.where(lane == 2, slope * 128.0,
                                             jnp.where(lane == 3, slope, 0.0))))
        q1 = jnp.where(lane < HEAD_DIM, q, 0.0).astype(BF16)
        q2 = jnp.where(lane >= HEAD_DIM, q, 0.0).astype(BF16)
        for var, sign in enumerate((1.0, -1.0, 0.0)):
            aug = (sign * base).astype(BF16)
            qs_ref[var, 0:tq, 0:LANES] = q1
            qs_ref[var, 0:tq, LANES:2 * LANES] = aug
            qs_ref[var, tq:2 * tq, 0:LANES] = q2
            qs_ref[var, tq:2 * tq, LANES:2 * LANES] = aug
        m_ref[...] = jnp.full(m_ref.shape, NEG, F32)
        acc_ref[...] = jnp.zeros(acc_ref.shape, F32)

    def step(diag):
        sel = jnp.where(ki < qi, 0, jnp.where(ki > qi, 1, 2))
        s = _dot_nt(qs_ref[sel], k_ref[...])
        if diag:
            r = lax.broadcasted_iota(jnp.int32, s.shape, 0) & (tq - 1)
            c = lax.broadcasted_iota(jnp.int32, s.shape, 1)
            s = s - slope_ref[:, 0:1] * jnp.abs(r - c).astype(F32)
        m_prev = m_ref[...]
        m_new = jnp.maximum(m_prev, jnp.max(s, axis=-1, keepdims=True))
        alpha = jnp.exp(m_prev - m_new)
        p = jnp.exp(s - jnp.concatenate([m_new] * (s.shape[1] // LANES), axis=1))
        pv = _dot(p.astype(BF16), v_ref[...])
        acc_ref[...] = acc_ref[...] * jnp.concatenate([alpha, alpha], axis=1) + pv
        m_ref[...] = m_new

    pl.when(ki != qi)(lambda: step(False))
    pl.when(ki == qi)(lambda: step(True))

    @pl.when(ki == nk - 1)
    def _fin():
        acc = acc_ref[...]
        o1 = acc[:tq, :LANES] / acc[:tq, LANES:]
        o2 = acc[tq:, :LANES] / acc[tq:, LANES:]
        lv = lam_ref[...]
        lam = (jnp.exp(jnp.sum(lv[0:1] * lv[1:2], axis=-1, keepdims=True))
               - jnp.exp(jnp.sum(lv[2:3] * lv[3:4], axis=-1, keepdims=True)) + lam_init)
        o = _rms(o1 - lam * o2, subln_ref[...], SUBLN_EPS) * (1.0 - lam_init)
        o_ref[...] = o.astype(BF16)


def _diff_attention(q, k, v, lam_vecs, subln, *, B, S, lam_init):
    tq = min(512, S)
    nq = S // tq
    slopes = np.asarray(_alibi_slopes(B_HEADS), np.float32)
    slope_tab = jnp.asarray(np.broadcast_to(slopes[:, None, None], (B_HEADS, 1, LANES)).copy())
    qv = q.reshape(B, S, B_QK)
    kv = k.reshape(B_HEADS, B, S, 2 * LANES)
    vv = v.reshape(B_HEADS, B, S, 2 * LANES)
    out = pl.pallas_call(
        functools.partial(_diff_kernel, tq=tq, nk=nq, lam_init=lam_init),
        grid=(B, B_HEADS, nq, nq),
        in_specs=[pl.BlockSpec((None, 1, LANES), lambda b, h, i, j: (h, 0, 0)),
                  pl.BlockSpec((4, HEAD_DIM), lambda b, h, i, j: (0, 0)),
                  pl.BlockSpec((1, B_VDIM), lambda b, h, i, j: (0, 0)),
                  pl.BlockSpec((None, tq, LANES), lambda b, h, i, j: (b, i, h)),
                  pl.BlockSpec((None, None, tq, 2 * LANES), lambda b, h, i, j: (h, b, j, 0)),
                  pl.BlockSpec((None, None, tq, 2 * LANES), lambda b, h, i, j: (h, b, j, 0))],
        out_specs=pl.BlockSpec((None, tq, LANES), lambda b, h, i, j: (b, i, h)),
        out_shape=jax.ShapeDtypeStruct((B, S, B_V), BF16),
        scratch_shapes=[pltpu.VMEM((3, 2 * tq, 2 * LANES), BF16),
                        pltpu.VMEM((2 * tq, LANES), F32),
                        pltpu.VMEM((2 * tq, 2 * LANES), F32)],
        compiler_params=pltpu.CompilerParams(
            dimension_semantics=("parallel", "parallel", "parallel", "arbitrary"),
            vmem_limit_bytes=VMEM_LIMIT_BYTES),
    )(slope_tab, lam_vecs.astype(F32), subln.reshape(1, B_VDIM).astype(F32), qv, kv, vv)
    return out.reshape(B * S, B_V)


def _mixout_kernel(x_ref, oa_ref, ob_ref, g0_ref, g1_ref, g2_ref, l0_ref, l1_ref, l2_ref, w_ref, ln_ref,
                   xo_ref, h_ref):
    l0, l1, l2 = l0_ref[...], l1_ref[...], l2_ref[...]
    mx = jnp.maximum(jnp.maximum(l0, l1), l2)
    e0, e1, e2 = jnp.exp(l0 - mx), jnp.exp(l1 - mx), jnp.exp(l2 - mx)
    oc = (e0 * g0_ref[...].astype(F32) + e1 * g1_ref[...].astype(F32) + e2 * g2_ref[...].astype(F32)) / (e0 + e1 + e2)
    mix = (_dot(oa_ref[...], w_ref[0:A_Q, :])
           + _dot(ob_ref[...], w_ref[A_Q:A_Q + B_V, :])
           + _dot(oc.astype(BF16), w_ref[A_Q + B_V:, :]))
    x = x_ref[...] + mix
    xo_ref[...] = x
    h_ref[...] = _rms(x, ln_ref[...], EPS).astype(BF16)


def _mix_out(x, oa, ob, og, lse, w_out, ln2):
    T = x.shape[0]
    tm = min(512, T)
    row = lambda i: (i, 0)
    const = lambda i: (0, 0)
    return pl.pallas_call(
        _mixout_kernel,
        grid=(T // tm,),
        in_specs=[pl.BlockSpec((tm, D_MODEL), row),
                  pl.BlockSpec((tm, A_Q), row),
                  pl.BlockSpec((tm, B_V), row)]
                 + [pl.BlockSpec((tm, HEADS_W), row)] * 6
                 + [pl.BlockSpec((D_MODEL, D_MODEL), const),
                    pl.BlockSpec((1, D_MODEL), const)],
        out_specs=[pl.BlockSpec((tm, D_MODEL), row), pl.BlockSpec((tm, D_MODEL), row)],
        out_shape=[jax.ShapeDtypeStruct((T, D_MODEL), F32), jax.ShapeDtypeStruct((T, D_MODEL), BF16)],
        compiler_params=pltpu.CompilerParams(dimension_semantics=("parallel",),
                                             vmem_limit_bytes=VMEM_LIMIT_BYTES),
    )(x, oa, ob, *og, *lse, w_out, ln2.reshape(1, D_MODEL))


def _matmul_kernel(a_ref, w_ref, o_ref):
    o_ref[...] = _dot(a_ref[...], w_ref[...]).astype(o_ref.dtype)


def _matmul(a, w, tn):
    T, K = a.shape
    N = w.shape[1]
    tm = min(1024, T)
    return pl.pallas_call(
        _matmul_kernel,
        grid=(T // tm, N // tn),
        in_specs=[pl.BlockSpec((tm, K), lambda i, j: (i, 0)),
                  pl.BlockSpec((K, tn), lambda i, j: (0, j))],
        out_specs=pl.BlockSpec((tm, tn), lambda i, j: (i, j)),
        out_shape=jax.ShapeDtypeStruct((T, N), BF16),
        compiler_params=pltpu.CompilerParams(dimension_semantics=("parallel", "parallel"),
                                             vmem_limit_bytes=VMEM_LIMIT_BYTES),
    )(a, w)


_CONV_CHUNK = 256


def _convdown_kernel(u_ref, up_ref, un_ref, cw_ref, cb_ref, wd_ref, x_ref, lnf_ref, o_ref, act_ref,
                     *, tm, seq_len, final_norm):
    pos0 = (pl.program_id(0) * tm) % seq_len
    keep_prev = jnp.where(pos0 == 0, 0.0, 1.0)
    keep_next = jnp.where(pos0 + tm == seq_len, 0.0, 1.0)
    row = lax.broadcasted_iota(jnp.int32, (tm, _CONV_CHUNK), 0)

    def conv(c0):
        cols = slice(c0, c0 + _CONV_CHUNK)
        uc = u_ref[:, cols].astype(F32)
        before = up_ref[SUBLANES - 1:SUBLANES, cols].astype(F32) * keep_prev
        after = un_ref[0:1, cols].astype(F32) * keep_next
        u_prev = jnp.where(row == 0, before, pltpu.roll(uc, 1, axis=0))
        u_next = jnp.where(row == tm - 1, after, pltpu.roll(uc, tm - 1, axis=0))
        return (u_prev * cw_ref[0:1, cols] + uc * cw_ref[1:2, cols] + u_next * cw_ref[2:3, cols]
                + cb_ref[:, cols])

    for c0 in range(0, D_FF, _CONV_CHUNK):
        gate = conv(c0)
        val = conv(D_FF + c0)
        act_ref[:, c0:c0 + _CONV_CHUNK] = (gate * jax.nn.sigmoid(gate) * val).astype(BF16)
    x = x_ref[...] + _dot(act_ref[...], wd_ref[...])
    if final_norm:
        x = _rms(x, lnf_ref[...], EPS)
    o_ref[...] = x


def _conv_down(u, conv_w, conv_b, w_down, x, ln_f, *, seq_len, final_norm):
    T = x.shape[0]
    tm = min(512, T)
    nh = tm // SUBLANES
    n8 = T // SUBLANES
    row = lambda i: (i, 0)
    const = lambda i: (0, 0)
    return pl.pallas_call(
        functools.partial(_convdown_kernel, tm=tm, seq_len=seq_len, final_norm=final_norm),
        grid=(T // tm,),
        in_specs=[pl.BlockSpec((tm, 2 * D_FF), row),
                  pl.BlockSpec((SUBLANES, 2 * D_FF), lambda i: (jnp.maximum(i * nh - 1, 0), 0)),
                  pl.BlockSpec((SUBLANES, 2 * D_FF), lambda i: (jnp.minimum((i + 1) * nh, n8 - 1), 0)),
                  pl.BlockSpec((3, 2 * D_FF), const),
                  pl.BlockSpec((1, 2 * D_FF), const),
                  pl.BlockSpec((D_FF, D_MODEL), const),
                  pl.BlockSpec((tm, D_MODEL), row),
                  pl.BlockSpec((1, D_MODEL), const)],
        out_specs=pl.BlockSpec((tm, D_MODEL), row),
        out_shape=jax.ShapeDtypeStruct((T, D_MODEL), F32),
        scratch_shapes=[pltpu.VMEM((tm, D_FF), BF16)],
        compiler_params=pltpu.CompilerParams(dimension_semantics=("parallel",),
                                             vmem_limit_bytes=VMEM_LIMIT_BYTES),
    )(u, u, u, conv_w, conv_b.reshape(1, 2 * D_FF), w_down, x, ln_f.reshape(1, D_MODEL))


def _trunk(x, p):
    B, S = x.shape[0], x.shape[1]
    xf = x.reshape(B * S, D_MODEL)
    a_slopes = _alibi_slopes(A_HEADS)
    c_slopes = _alibi_slopes(C_GROUPS * C_HEADS)
    for l in range(DEPTH):
        aq, ak, av, bq, bk, bv, cq, ck, cv = _in_proj(xf, p["ln1"][l], p["w_in"][l], S)
        (oa,) = _banded_attention(aq, ak, av, B=B, S=S, d=1, q_cb=0, k_cb=0, v_cb=0, window=A_WINDOW,
                                  slopes_eff=a_slopes, sink=p["a_sink"][l])
        lam_init = 0.8 - 0.6 * math.exp(-0.3 * l)
        ob = _diff_attention(bq, bk, bv, p["lam"][l], p["subln"][l], B=B, S=S, lam_init=lam_init)
        og, lse = [], []
        for g, (w, d) in enumerate(C_PAIRS):
            o_g, lse_g = _banded_attention(
                cq, ck, cv, B=B, S=S, d=d, q_cb=g, k_cb=g, v_cb=g, window=w // (2 * d),
                slopes_eff=[s * d for s in c_slopes[g * C_HEADS:(g + 1) * C_HEADS]], emit_lse=True)
            og.append(o_g)
            lse.append(lse_g)
        xf, h2 = _mix_out(xf, oa, ob, og, lse, p["w_out"][l], p["ln2"][l])
        u = _matmul(h2, p["w_up"][l], D_FF)
        xf = _conv_down(u, p["conv_w"][l], p["conv_b"][l], p["w_down"][l], xf, p["ln_f"],
                        seq_len=S, final_norm=(l == DEPTH - 1))
    return xf.reshape(B, S, D_MODEL)


def kernel(x_prompt, x_sample, ln1, w_in, a_sink, lam_q1, lam_k1, lam_q2, lam_k2, subln, w_out, ln2, w_up,
           conv_w, conv_b, w_down, ln_f):
    p = {
        "ln1": ln1,
        "w_in": [_prep_w_in(w_in[l]) for l in range(DEPTH)],
        "a_sink": a_sink,
        "lam": jnp.stack([lam_q1, lam_k1, lam_q2, lam_k2], axis=1),
        "subln": subln,
        "w_out": w_out.astype(BF16),
        "ln2": ln2,
        "w_up": w_up.astype(BF16),
        "conv_w": conv_w,
        "conv_b": conv_b,
        "w_down": w_down.astype(BF16),
        "ln_f": ln_f,
    }
    return (_trunk(x_prompt, p), _trunk(x_sample, p))
```

```python
import functools
import math

import numpy as np
import jax
import jax.numpy as jnp
from jax import lax
from jax.experimental import pallas as pl
from jax.experimental.pallas import tpu as pltpu

F32 = jnp.float32
BF16 = jnp.bfloat16

D_MODEL = 1024
DEPTH = 2
HEAD_DIM = 64
BLOCK = 128
EPS = 1e-6
SUBLN_EPS = 1e-5
NEG = -1e30
SCALE = HEAD_DIM ** -0.5
A_HEADS = 4
A_KV_HEADS = 2
A_WINDOW = 128
B_HEADS = 4
B_VDIM = 2 * HEAD_DIM
C_PAIRS = ((128, 1), (512, 4), (2048, 16))
C_GROUPS = 3
C_HEADS = 4
D_FF = 2816
A_Q = A_HEADS * HEAD_DIM
A_KV = A_KV_HEADS * HEAD_DIM
B_QK = B_HEADS * 2 * HEAD_DIM
B_V = B_HEADS * B_VDIM
C_QKV = C_GROUPS * C_HEADS * HEAD_DIM
HEADS_W = 4 * HEAD_DIM

VMEM_LIMIT_BYTES = 56 * 1024 * 1024
LANES = 128
SUBLANES = 8


def _alibi_slopes(n):
    return [2.0 ** (-8.0 * k / n) for k in range(1, n + 1)]


def _dot(a, b):
    return jnp.dot(a, b, preferred_element_type=F32)


def _dot_nt(a, b):
    return lax.dot_general(a, b, (((1,), (1,)), ((), ())), preferred_element_type=F32)


def _rms(x, g, eps):
    return x * lax.rsqrt(jnp.mean(x * x, axis=-1, keepdims=True) + eps) * g


_W_AQ, _W_AK, _W_AV = 0, 256, 512
_W_BQ, _W_BK, _W_BV = 768, 1280, 1792
_W_CQ, _W_CK, _W_CV = 2304, 3072, 3840
_W_COLS = 4608


def _inproj_kernel(x_ref, g_ref, w_ref, aq_ref, ak_ref, av_ref, bq_ref, bk_ref, bv_ref,
                   cq_ref, ck_ref, cv_ref, *, seq_len, tm):
    h = _rms(x_ref[...], g_ref[...], EPS).astype(BF16)

    def proj(c0, c1):
        return _dot(h, w_ref[:, c0:c1])

    aq_ref[...] = (proj(_W_AQ, _W_AK) * SCALE).astype(BF16)
    ak_ref[...] = proj(_W_AK, _W_AV).astype(BF16)
    av_ref[...] = proj(_W_AV, _W_BQ).astype(BF16)
    bq_ref[...] = (proj(_W_BQ, _W_BK) * SCALE).astype(BF16)
    pos = (pl.program_id(0) * tm + lax.broadcasted_iota(jnp.int32, (tm, LANES), 0)) % seq_len
    lane = lax.broadcasted_iota(jnp.int32, (tm, LANES), 1)
    aug = jnp.where(lane < 2, 1.0,
                    jnp.where(lane == 2, (pos >> 7).astype(F32),
                              jnp.where(lane == 3, (pos & 127).astype(F32), 0.0))).astype(BF16)
    ones = jnp.ones((tm, LANES), BF16)
    bk = proj(_W_BK, _W_BV).astype(BF16)
    bv = proj(_W_BV, _W_CQ).astype(BF16)
    for hh in range(B_HEADS):
        bk_ref[hh, :, 0:LANES] = bk[:, hh * LANES:(hh + 1) * LANES]
        bk_ref[hh, :, LANES:2 * LANES] = aug
        bv_ref[hh, :, 0:LANES] = bv[:, hh * LANES:(hh + 1) * LANES]
        bv_ref[hh, :, LANES:2 * LANES] = ones
    cq_ref[...] = (proj(_W_CQ, _W_CK) * SCALE).astype(BF16)
    ck_ref[...] = proj(_W_CK, _W_CV).astype(BF16)
    cv_ref[...] = proj(_W_CV, _W_COLS).astype(BF16)


def _in_proj(x, g, w, seq_len):
    T = x.shape[0]
    tm = min(512, T)
    row = lambda i: (i, 0)
    const = lambda i: (0, 0)
    outs = [
        (jax.ShapeDtypeStruct((T, 256), BF16), pl.BlockSpec((tm, 256), row)),
        (jax.ShapeDtypeStruct((T, 256), BF16), pl.BlockSpec((tm, 256), row)),
        (jax.ShapeDtypeStruct((T, 256), BF16), pl.BlockSpec((tm, 256), row)),
        (jax.ShapeDtypeStruct((T, B_QK), BF16), pl.BlockSpec((tm, B_QK), row)),
        (jax.ShapeDtypeStruct((B_HEADS, T, 256), BF16), pl.BlockSpec((B_HEADS, tm, 256), lambda i: (0, i, 0))),
        (jax.ShapeDtypeStruct((B_HEADS, T, 256), BF16), pl.BlockSpec((B_HEADS, tm, 256), lambda i: (0, i, 0))),
        (jax.ShapeDtypeStruct((T, C_QKV), BF16), pl.BlockSpec((tm, C_QKV), row)),
        (jax.ShapeDtypeStruct((T, C_QKV), BF16), pl.BlockSpec((tm, C_QKV), row)),
        (jax.ShapeDtypeStruct((T, C_QKV), BF16), pl.BlockSpec((tm, C_QKV), row)),
    ]
    return pl.pallas_call(
        functools.partial(_inproj_kernel, seq_len=seq_len, tm=tm),
        grid=(T // tm,),
        in_specs=[pl.BlockSpec((tm, D_MODEL), row),
                  pl.BlockSpec((1, D_MODEL), const),
                  pl.BlockSpec((D_MODEL, _W_COLS), const)],
        out_specs=[o[1] for o in outs],
        out_shape=[o[0] for o in outs],
        compiler_params=pltpu.CompilerParams(dimension_semantics=("parallel",),
                                             vmem_limit_bytes=VMEM_LIMIT_BYTES),
    )(x, g.reshape(1, D_MODEL), w)


def _prep_w_in(w):
    g = A_HEADS // A_KV_HEADS
    aq = w[:, :A_Q]
    ak = jnp.repeat(w[:, A_Q:A_Q + A_KV].reshape(D_MODEL, A_KV_HEADS, 1, HEAD_DIM), g, axis=2).reshape(D_MODEL, A_Q)
    av = jnp.repeat(w[:, A_Q + A_KV:A_Q + 2 * A_KV].reshape(D_MODEL, A_KV_HEADS, 1, HEAD_DIM), g, axis=2).reshape(D_MODEL, A_Q)
    return jnp.concatenate([aq, ak, av, w[:, A_Q + 2 * A_KV:]], axis=1).astype(BF16)


def _banded_kernel(*refs, nb, has_sink, emit_lse):
    q_ref, kp_ref, kc_ref, kn_ref, vp_ref, vc_ref, vn_ref, bias_ref = refs[:8]
    rest = refs[8:]
    if has_sink:
        sink_ref, rest = rest[0], rest[1:]
    o_ref = rest[0]
    lse_ref = rest[1] if emit_lse else None

    ib = pl.program_id(2)
    q = q_ref[...].astype(F32)
    kb = jnp.concatenate([kp_ref[...], kc_ref[...], kn_ref[...]], axis=0)
    vb = jnp.concatenate([vp_ref[...], vc_ref[...], vn_ref[...]], axis=0)
    col = lax.broadcasted_iota(jnp.int32, (1, 3 * BLOCK), 1)
    off_seq = ((ib == 0) & (col < BLOCK)) | ((ib == nb - 1) & (col >= 2 * BLOCK))
    edge = jnp.where(off_seq, NEG, 0.0)
    lane_head = lax.broadcasted_iota(jnp.int32, (BLOCK, HEADS_W), 1) // HEAD_DIM
    o_acc = jnp.zeros((BLOCK, HEADS_W), F32)
    lse_acc = jnp.zeros((BLOCK, HEADS_W), F32)
    for h in range(4):
        mine = lane_head == h
        qh = jnp.where(mine, q, 0.0).astype(BF16)
        s = _dot_nt(qh, kb) + bias_ref[h] + edge
        m = jnp.max(s, axis=-1, keepdims=True)
        if has_sink:
            sk = sink_ref[h]
            m = jnp.maximum(m, sk)
        p = jnp.exp(s - m)
        l = jnp.sum(p, axis=-1, keepdims=True)
        if has_sink:
            l = l + jnp.exp(sk - m)
        pv = _dot(p.astype(BF16), vb)
        o_acc = jnp.where(mine, pv / l, o_acc)
        if emit_lse:
            lse_acc = jnp.where(mine, m + jnp.log(l), lse_acc)
    o_ref[...] = o_acc.astype(BF16)
    if emit_lse:
        lse_ref[...] = lse_acc


def _band_bias(window, slopes_eff):
    il = np.arange(BLOCK)[:, None]
    c = np.arange(3 * BLOCK)[None, :]
    dist = np.abs(c - BLOCK - il).astype(np.float64)
    tabs = [np.where(dist <= window, -s * dist, NEG) for s in slopes_eff]
    return jnp.asarray(np.stack(tabs), dtype=F32)


def _banded_attention(q, k, v, *, B, S, d, q_cb, k_cb, v_cb, window, slopes_eff, sink=None, emit_lse=False):
    Sd = S // d
    nb = Sd // BLOCK
    nq, nk, nv = q.shape[1] // HEADS_W, k.shape[1] // HEADS_W, v.shape[1] // HEADS_W
    qv = q.reshape(B, Sd, d * q.shape[1])
    kv = k.reshape(B, Sd, d * k.shape[1])
    vv = v.reshape(B, Sd, d * v.shape[1])
    blk = (None, BLOCK, HEADS_W)

    def spec(ncb, cb, shift):
        def imap(b, r, i):
            return (b, jnp.clip(i + shift, 0, nb - 1), r * ncb + cb)
        return pl.BlockSpec(blk, imap)

    in_specs = [spec(nq, q_cb, 0),
                spec(nk, k_cb, -1), spec(nk, k_cb, 0), spec(nk, k_cb, 1),
                spec(nv, v_cb, -1), spec(nv, v_cb, 0), spec(nv, v_cb, 1),
                pl.BlockSpec((4, BLOCK, 3 * BLOCK), lambda b, r, i: (0, 0, 0))]
    args = [qv, kv, kv, kv, vv, vv, vv, _band_bias(window, slopes_eff)]
    if sink is not None:
        in_specs.append(pl.BlockSpec(memory_space=pltpu.SMEM))
        args.append(sink.astype(F32))
    out_spec = pl.BlockSpec(blk, lambda b, r, i: (b, i, r))
    out_shape = [jax.ShapeDtypeStruct((B, Sd, d * HEADS_W), BF16)]
    out_specs = [out_spec]
    if emit_lse:
        out_shape.append(jax.ShapeDtypeStruct((B, Sd, d * HEADS_W), F32))
        out_specs.append(out_spec)
    res = pl.pallas_call(
        functools.partial(_banded_kernel, nb=nb, has_sink=sink is not None, emit_lse=emit_lse),
        grid=(B, d, nb),
        in_specs=in_specs,
        out_specs=out_specs,
        out_shape=out_shape,
        compiler_params=pltpu.CompilerParams(dimension_semantics=("parallel", "parallel", "parallel"),
                                             vmem_limit_bytes=VMEM_LIMIT_BYTES),
    )(*args)
    return [r.reshape(B * S, HEADS_W) for r in res]


_SOFTMAX_ROWS = 128


def _diff_kernel(slope_ref, lam_ref, subln_ref, q_ref, k_ref, v_ref, o_ref,
                 qs_ref, m_ref, acc_ref, s0_ref, s1_ref, p0_ref, p1_ref, a0_ref, a1_ref,
                 *, tq, nk, lam_init):
    qi = pl.program_id(2)
    s_refs, p_refs, a_refs = (s0_ref, s1_ref), (p0_ref, p1_ref), (a0_ref, a1_ref)

    q = q_ref[...].astype(F32)
    lane = lax.broadcasted_iota(jnp.int32, (tq, LANES), 1)
    pos = qi * tq + lax.broadcasted_iota(jnp.int32, (tq, LANES), 0)
    slope = slope_ref[...]
    base = jnp.where(lane == 0, -slope * 128.0 * (pos >> 7).astype(F32),
                     jnp.where(lane == 1, -slope * (pos & 127).astype(F32),
                               jnp.where(lane == 2, slope * 128.0,
                                         jnp.where(lane == 3, slope, 0.0))))
    q1 = jnp.where(lane < HEAD_DIM, q, 0.0).astype(BF16)
    q2 = jnp.where(lane >= HEAD_DIM, q, 0.0).astype(BF16)
    for var, sign in enumerate((1.0, -1.0, 0.0)):
        aug = (sign * base).astype(BF16)
        qs_ref[var, 0:tq, 0:LANES] = q1
        qs_ref[var, 0:tq, LANES:2 * LANES] = aug
        qs_ref[var, tq:2 * tq, 0:LANES] = q2
        qs_ref[var, tq:2 * tq, LANES:2 * LANES] = aug
    m_ref[...] = jnp.full(m_ref.shape, NEG, F32)
    acc_ref[...] = jnp.zeros(acc_ref.shape, F32)

    def key_tile(t):
        return jnp.where(t == 0, qi, jnp.where(t <= qi, t - 1, t))

    def rows_of(j):
        return pl.ds(pl.multiple_of(j * tq, tq), tq)

    def stage1_diag(slot):
        s = _dot_nt(qs_ref[2], k_ref[rows_of(qi), :])
        r = lax.broadcasted_iota(jnp.int32, s.shape, 0) & (tq - 1)
        c = lax.broadcasted_iota(jnp.int32, s.shape, 1)
        s_refs[slot][...] = s - slope_ref[:, 0:1] * jnp.abs(r - c).astype(F32)

    def stage1(t, slot):
        j = key_tile(t)
        s_refs[slot][...] = _dot_nt(qs_ref[jnp.where(j < qi, 0, 1)], k_ref[rows_of(j), :])

    def stage2(slot):
        for r0 in range(0, 2 * tq, _SOFTMAX_ROWS):
            rows = slice(r0, r0 + _SOFTMAX_ROWS)
            s = s_refs[slot][rows, :]
            m_prev = m_ref[rows, :]
            m_new = jnp.maximum(m_prev, jnp.max(s, axis=-1, keepdims=True))
            a_refs[slot][rows, :] = jnp.exp(m_prev - m_new)
            p = jnp.exp(s - jnp.concatenate([m_new] * (tq // LANES), axis=1))
            p_refs[slot][rows, :] = p.astype(BF16)
            m_ref[rows, :] = m_new

    def stage3(j, slot):
        pv = _dot(p_refs[slot][...], v_ref[rows_of(j), :])
        a = a_refs[slot][...]
        acc_ref[...] = acc_ref[...] * jnp.concatenate([a, a], axis=1) + pv

    stage1_diag(0)
    stage1(1, 1)
    stage2(0)

    def steady(u, carry):
        t = 2 + 2 * u
        stage1(t, 0)
        stage3(key_tile(t - 2), 0)
        stage2(1)
        stage1(t + 1, 1)
        stage3(key_tile(t - 1), 1)
        stage2(0)
        return carry

    lax.fori_loop(0, (nk - 2) // 2, steady, 0)
    stage3(key_tile(nk - 2), 0)
    stage2(1)
    stage3(key_tile(nk - 1), 1)

    acc = acc_ref[...]
    o1 = acc[:tq, :LANES] / acc[:tq, LANES:]
    o2 = acc[tq:, :LANES] / acc[tq:, LANES:]
    lv = lam_ref[...]
    lam = (jnp.exp(jnp.sum(lv[0:1] * lv[1:2], axis=-1, keepdims=True))
           - jnp.exp(jnp.sum(lv[2:3] * lv[3:4], axis=-1, keepdims=True)) + lam_init)
    o = _rms(o1 - lam * o2, subln_ref[...], SUBLN_EPS) * (1.0 - lam_init)
    o_ref[...] = o.astype(BF16)


def _diff_attention(q, k, v, lam_vecs, subln, *, B, S, lam_init):
    tq = min(512, S // 2)
    nq = S // tq
    assert nq >= 2 and nq % 2 == 0
    slopes = np.asarray(_alibi_slopes(B_HEADS), np.float32)
    slope_tab = jnp.asarray(np.broadcast_to(slopes[:, None, None], (B_HEADS, 1, LANES)).copy())
    qv = q.reshape(B, S, B_QK)
    kv = k.reshape(B_HEADS, B, S, 2 * LANES)
    vv = v.reshape(B_HEADS, B, S, 2 * LANES)
    out = pl.pallas_call(
        functools.partial(_diff_kernel, tq=tq, nk=nq, lam_init=lam_init),
        grid=(B, B_HEADS, nq),
        in_specs=[pl.BlockSpec((None, 1, LANES), lambda b, h, i: (h, 0, 0)),
                  pl.BlockSpec((4, HEAD_DIM), lambda b, h, i: (0, 0)),
                  pl.BlockSpec((1, B_VDIM), lambda b, h, i: (0, 0)),
                  pl.BlockSpec((None, tq, LANES), lambda b, h, i: (b, i, h)),
                  pl.BlockSpec((None, None, S, 2 * LANES), lambda b, h, i: (h, b, 0, 0)),
                  pl.BlockSpec((None, None, S, 2 * LANES), lambda b, h, i: (h, b, 0, 0))],
        out_specs=pl.BlockSpec((None, tq, LANES), lambda b, h, i: (b, i, h)),
        out_shape=jax.ShapeDtypeStruct((B, S, B_V), BF16),
        scratch_shapes=[pltpu.VMEM((3, 2 * tq, 2 * LANES), BF16),
                        pltpu.VMEM((2 * tq, LANES), F32),
                        pltpu.VMEM((2 * tq, 2 * LANES), F32),
                        pltpu.VMEM((2 * tq, tq), F32), pltpu.VMEM((2 * tq, tq), F32),
                        pltpu.VMEM((2 * tq, tq), BF16), pltpu.VMEM((2 * tq, tq), BF16),
                        pltpu.VMEM((2 * tq, LANES), F32), pltpu.VMEM((2 * tq, LANES), F32)],
        compiler_params=pltpu.CompilerParams(
            dimension_semantics=("parallel", "parallel", "arbitrary"),
            vmem_limit_bytes=VMEM_LIMIT_BYTES),
    )(slope_tab, lam_vecs.astype(F32), subln.reshape(1, B_VDIM).astype(F32), qv, kv, vv)
    return out.reshape(B * S, B_V)


def _mixout_kernel(x_ref, oa_ref, ob_ref, g0_ref, g1_ref, g2_ref, l0_ref, l1_ref, l2_ref, w_ref, ln_ref,
                   xo_ref, h_ref):
    l0, l1, l2 = l0_ref[...], l1_ref[...], l2_ref[...]
    mx = jnp.maximum(jnp.maximum(l0, l1), l2)
    e0, e1, e2 = jnp.exp(l0 - mx), jnp.exp(l1 - mx), jnp.exp(l2 - mx)
    oc = (e0 * g0_ref[...].astype(F32) + e1 * g1_ref[...].astype(F32) + e2 * g2_ref[...].astype(F32)) / (e0 + e1 + e2)
    mix = (_dot(oa_ref[...], w_ref[0:A_Q, :])
           + _dot(ob_ref[...], w_ref[A_Q:A_Q + B_V, :])
           + _dot(oc.astype(BF16), w_ref[A_Q + B_V:, :]))
    x = x_ref[...] + mix
    xo_ref[...] = x
    h_ref[...] = _rms(x, ln_ref[...], EPS).astype(BF16)


def _mix_out(x, oa, ob, og, lse, w_out, ln2):
    T = x.shape[0]
    tm = min(512, T)
    row = lambda i: (i, 0)
    const = lambda i: (0, 0)
    return pl.pallas_call(
        _mixout_kernel,
        grid=(T // tm,),
        in_specs=[pl.BlockSpec((tm, D_MODEL), row),
                  pl.BlockSpec((tm, A_Q), row),
                  pl.BlockSpec((tm, B_V), row)]
                 + [pl.BlockSpec((tm, HEADS_W), row)] * 6
                 + [pl.BlockSpec((D_MODEL, D_MODEL), const),
                    pl.BlockSpec((1, D_MODEL), const)],
        out_specs=[pl.BlockSpec((tm, D_MODEL), row), pl.BlockSpec((tm, D_MODEL), row)],
        out_shape=[jax.ShapeDtypeStruct((T, D_MODEL), F32), jax.ShapeDtypeStruct((T, D_MODEL), BF16)],
        compiler_params=pltpu.CompilerParams(dimension_semantics=("parallel",),
                                             vmem_limit_bytes=VMEM_LIMIT_BYTES),
    )(x, oa, ob, *og, *lse, w_out, ln2.reshape(1, D_MODEL))


def _matmul_kernel(a_ref, w_ref, o_ref):
    o_ref[...] = _dot(a_ref[...], w_ref[...]).astype(o_ref.dtype)


def _matmul(a, w, tn):
    T, K = a.shape
    N = w.shape[1]
    tm = min(1024, T)
    return pl.pallas_call(
        _matmul_kernel,
        grid=(T // tm, N // tn),
        in_specs=[pl.BlockSpec((tm, K), lambda i, j: (i, 0)),
                  pl.BlockSpec((K, tn), lambda i, j: (0, j))],
        out_specs=pl.BlockSpec((tm, tn), lambda i, j: (i, j)),
        out_shape=jax.ShapeDtypeStruct((T, N), BF16),
        compiler_params=pltpu.CompilerParams(dimension_semantics=("parallel", "parallel"),
                                             vmem_limit_bytes=VMEM_LIMIT_BYTES),
    )(a, w)


_CONV_CHUNK = 256


def _convdown_kernel(u_ref, up_ref, un_ref, cw_ref, cb_ref, wd_ref, x_ref, lnf_ref, o_ref, act_ref,
                     *, tm, seq_len, final_norm):
    pos0 = (pl.program_id(0) * tm) % seq_len
    keep_prev = jnp.where(pos0 == 0, 0.0, 1.0)
    keep_next = jnp.where(pos0 + tm == seq_len, 0.0, 1.0)
    row = lax.broadcasted_iota(jnp.int32, (tm, _CONV_CHUNK), 0)

    def conv(c0):
        cols = slice(c0, c0 + _CONV_CHUNK)
        uc = u_ref[:, cols].astype(F32)
        before = up_ref[SUBLANES - 1:SUBLANES, cols].astype(F32) * keep_prev
        after = un_ref[0:1, cols].astype(F32) * keep_next
        u_prev = jnp.where(row == 0, before, pltpu.roll(uc, 1, axis=0))
        u_next = jnp.where(row == tm - 1, after, pltpu.roll(uc, tm - 1, axis=0))
        return (u_prev * cw_ref[0:1, cols] + uc * cw_ref[1:2, cols] + u_next * cw_ref[2:3, cols]
                + cb_ref[:, cols])

    for c0 in range(0, D_FF, _CONV_CHUNK):
        gate = conv(c0)
        val = conv(D_FF + c0)
        act_ref[:, c0:c0 + _CONV_CHUNK] = (gate * jax.nn.sigmoid(gate) * val).astype(BF16)
    x = x_ref[...] + _dot(act_ref[...], wd_ref[...])
    if final_norm:
        x = _rms(x, lnf_ref[...], EPS)
    o_ref[...] = x


def _conv_down(u, conv_w, conv_b, w_down, x, ln_f, *, seq_len, final_norm):
    T = x.shape[0]
    tm = min(512, T)
    nh = tm // SUBLANES
    n8 = T // SUBLANES
    row = lambda i: (i, 0)
    const = lambda i: (0, 0)
    return pl.pallas_call(
        functools.partial(_convdown_kernel, tm=tm, seq_len=seq_len, final_norm=final_norm),
        grid=(T // tm,),
        in_specs=[pl.BlockSpec((tm, 2 * D_FF), row),
                  pl.BlockSpec((SUBLANES, 2 * D_FF), lambda i: (jnp.maximum(i * nh - 1, 0), 0)),
                  pl.BlockSpec((SUBLANES, 2 * D_FF), lambda i: (jnp.minimum((i + 1) * nh, n8 - 1), 0)),
                  pl.BlockSpec((3, 2 * D_FF), const),
                  pl.BlockSpec((1, 2 * D_FF), const),
                  pl.BlockSpec((D_FF, D_MODEL), const),
                  pl.BlockSpec((tm, D_MODEL), row),
                  pl.BlockSpec((1, D_MODEL), const)],
        out_specs=pl.BlockSpec((tm, D_MODEL), row),
        out_shape=jax.ShapeDtypeStruct((T, D_MODEL), F32),
        scratch_shapes=[pltpu.VMEM((tm, D_FF), BF16)],
        compiler_params=pltpu.CompilerParams(dimension_semantics=("parallel",),
                                             vmem_limit_bytes=VMEM_LIMIT_BYTES),
    )(u, u, u, conv_w, conv_b.reshape(1, 2 * D_FF), w_down, x, ln_f.reshape(1, D_MODEL))


def _trunk(x, p):
    B, S = x.shape[0], x.shape[1]
    xf = x.reshape(B * S, D_MODEL)
    a_slopes = _alibi_slopes(A_HEADS)
    c_slopes = _alibi_slopes(C_GROUPS * C_HEADS)
    for l in range(DEPTH):
        aq, ak, av, bq, bk, bv, cq, ck, cv = _in_proj(xf, p["ln1"][l], p["w_in"][l], S)
        (oa,) = _banded_attention(aq, ak, av, B=B, S=S, d=1, q_cb=0, k_cb=0, v_cb=0, window=A_WINDOW,
                                  slopes_eff=a_slopes, sink=p["a_sink"][l])
        lam_init = 0.8 - 0.6 * math.exp(-0.3 * l)
        ob = _diff_attention(bq, bk, bv, p["lam"][l], p["subln"][l], B=B, S=S, lam_init=lam_init)
        og, lse = [], []
        for g, (w, d) in enumerate(C_PAIRS):
            o_g, lse_g = _banded_attention(
                cq, ck, cv, B=B, S=S, d=d, q_cb=g, k_cb=g, v_cb=g, window=w // (2 * d),
                slopes_eff=[s * d for s in c_slopes[g * C_HEADS:(g + 1) * C_HEADS]], emit_lse=True)
            og.append(o_g)
            lse.append(lse_g)
        xf, h2 = _mix_out(xf, oa, ob, og, lse, p["w_out"][l], p["ln2"][l])
        u = _matmul(h2, p["w_up"][l], D_FF)
        xf = _conv_down(u, p["conv_w"][l], p["conv_b"][l], p["w_down"][l], xf, p["ln_f"],
                        seq_len=S, final_norm=(l == DEPTH - 1))
    return xf.reshape(B, S, D_MODEL)


def kernel(x_prompt, x_sample, ln1, w_in, a_sink, lam_q1, lam_k1, lam_q2, lam_k2, subln, w_out, ln2, w_up,
           conv_w, conv_b, w_down, ln_f):
    p = {
        "ln1": ln1,
        "w_in": [_prep_w_in(w_in[l]) for l in range(DEPTH)],
        "a_sink": a_sink,
        "lam": jnp.stack([lam_q1, lam_k1, lam_q2, lam_k2], axis=1),
        "subln": subln,
        "w_out": w_out.astype(BF16),
        "ln2": ln2,
        "w_up": w_up.astype(BF16),
        "conv_w": conv_w,
        "conv_b": conv_b,
        "w_down": w_down.astype(BF16),
        "ln_f": ln_f,
    }
    return (_trunk(x_prompt, p), _trunk(x_sample, p))
```

```python
import functools
import math

import numpy as np
import jax
import jax.numpy as jnp
from jax import lax
from jax.experimental import pallas as pl
from jax.experimental.pallas import tpu as pltpu

F32 = jnp.float32
BF16 = jnp.bfloat16

D_MODEL = 1024
DEPTH = 2
HEAD_DIM = 64
BLOCK = 128
EPS = 1e-6
SUBLN_EPS = 1e-5
NEG = -1e30
SCALE = HEAD_DIM ** -0.5
A_HEADS = 4
A_KV_HEADS = 2
A_WINDOW = 128
B_HEADS = 4
B_VDIM = 2 * HEAD_DIM
C_PAIRS = ((128, 1), (512, 4), (2048, 16))
C_GROUPS = 3
C_HEADS = 4
D_FF = 2816
A_Q = A_HEADS * HEAD_DIM
A_KV = A_KV_HEADS * HEAD_DIM
B_QK = B_HEADS * 2 * HEAD_DIM
B_V = B_HEADS * B_VDIM
C_QKV = C_GROUPS * C_HEADS * HEAD_DIM
HEADS_W = 4 * HEAD_DIM

VMEM_LIMIT_BYTES = 56 * 1024 * 1024
LANES = 128
SUBLANES = 8


def _alibi_slopes(n):
    return [2.0 ** (-8.0 * k / n) for k in range(1, n + 1)]


def _dot(a, b):
    return jnp.dot(a, b, preferred_element_type=F32)


def _dot_nt(a, b):
    return lax.dot_general(a, b, (((1,), (1,)), ((), ())), preferred_element_type=F32)


def _rms(x, g, eps):
    return x * lax.rsqrt(jnp.mean(x * x, axis=-1, keepdims=True) + eps) * g


_W_AQ, _W_AK, _W_AV = 0, 256, 512
_W_BQ, _W_BK, _W_BV = 768, 1280, 1792
_W_CQ, _W_CK, _W_CV = 2304, 3072, 3840
_W_COLS = 4608


def _inproj_kernel(x_ref, g_ref, w_ref, aq_ref, ak_ref, av_ref, bq_ref, bk_ref, bv_ref, *rest, seq_len, tm):
    c_refs, slabs = rest[:9], rest[9:]
    h = _rms(x_ref[...], g_ref[...], EPS).astype(BF16)

    def proj(c0, c1):
        return _dot(h, w_ref[:, c0:c1])

    aq_ref[...] = (proj(_W_AQ, _W_AK) * SCALE).astype(BF16)
    ak_ref[...] = proj(_W_AK, _W_AV).astype(BF16)
    av_ref[...] = proj(_W_AV, _W_BQ).astype(BF16)
    bq_ref[...] = (proj(_W_BQ, _W_BK) * SCALE).astype(BF16)
    pos = (pl.program_id(0) * tm + lax.broadcasted_iota(jnp.int32, (tm, LANES), 0)) % seq_len
    lane = lax.broadcasted_iota(jnp.int32, (tm, LANES), 1)
    aug = jnp.where(lane < 2, 1.0,
                    jnp.where(lane == 2, (pos >> 7).astype(F32),
                              jnp.where(lane == 3, (pos & 127).astype(F32), 0.0))).astype(BF16)
    ones = jnp.ones((tm, LANES), BF16)
    bk = proj(_W_BK, _W_BV).astype(BF16)
    bv = proj(_W_BV, _W_CQ).astype(BF16)
    for hh in range(B_HEADS):
        bk_ref[hh, :, 0:LANES] = bk[:, hh * LANES:(hh + 1) * LANES]
        bk_ref[hh, :, LANES:2 * LANES] = aug
        bv_ref[hh, :, 0:LANES] = bv[:, hh * LANES:(hh + 1) * LANES]
        bv_ref[hh, :, LANES:2 * LANES] = ones
    slab = 0
    for part, (c0, scale) in enumerate(((_W_CQ, SCALE), (_W_CK, 1.0), (_W_CV, 1.0))):
        pc = proj(c0, c0 + C_QKV) * scale
        for g, (_, d) in enumerate(C_PAIRS):
            dst = c_refs[3 * part + g]
            val = pc[:, g * HEADS_W:(g + 1) * HEADS_W]
            if d == 1:
                dst[0] = val.astype(BF16)
            else:
                for c in range(HEADS_W // LANES):
                    buf = slabs[slab]
                    slab += 1
                    lanes = slice(c * LANES, (c + 1) * LANES)
                    buf[...] = val[:, lanes]
                    for r in range(d):
                        dst[r, :, lanes] = buf[pl.ds(r, tm // d, stride=d), :].astype(BF16)


def _in_proj(x, g, w, B, S):
    T = x.shape[0]
    tm = min(512, S)
    nt = S // tm
    row = lambda i: (i, 0)
    const = lambda i: (0, 0)
    outs = [
        (jax.ShapeDtypeStruct((T, 256), BF16), pl.BlockSpec((tm, 256), row)),
        (jax.ShapeDtypeStruct((T, 256), BF16), pl.BlockSpec((tm, 256), row)),
        (jax.ShapeDtypeStruct((T, 256), BF16), pl.BlockSpec((tm, 256), row)),
        (jax.ShapeDtypeStruct((T, B_QK), BF16), pl.BlockSpec((tm, B_QK), row)),
        (jax.ShapeDtypeStruct((B_HEADS, T, 256), BF16), pl.BlockSpec((B_HEADS, tm, 256), lambda i: (0, i, 0))),
        (jax.ShapeDtypeStruct((B_HEADS, T, 256), BF16), pl.BlockSpec((B_HEADS, tm, 256), lambda i: (0, i, 0))),
    ]
    for _ in range(3):
        for _, d in C_PAIRS:
            outs.append((jax.ShapeDtypeStruct((B, d, S // d, HEADS_W), BF16),
                         pl.BlockSpec((None, d, tm // d, HEADS_W), lambda i: (i // nt, 0, i % nt, 0))))
    n_slabs = 3 * sum(1 for _, d in C_PAIRS if d > 1)
    return pl.pallas_call(
        functools.partial(_inproj_kernel, seq_len=S, tm=tm),
        grid=(T // tm,),
        in_specs=[pl.BlockSpec((tm, D_MODEL), row),
                  pl.BlockSpec((1, D_MODEL), const),
                  pl.BlockSpec((D_MODEL, _W_COLS), const)],
        out_specs=[o[1] for o in outs],
        out_shape=[o[0] for o in outs],
        scratch_shapes=[pltpu.VMEM((tm, LANES), F32)] * (n_slabs * (HEADS_W // LANES)),
        compiler_params=pltpu.CompilerParams(dimension_semantics=("parallel",),
                                             vmem_limit_bytes=VMEM_LIMIT_BYTES),
    )(x, g.reshape(1, D_MODEL), w)


def _prep_w_in(w):
    g = A_HEADS // A_KV_HEADS
    aq = w[:, :A_Q]
    ak = jnp.repeat(w[:, A_Q:A_Q + A_KV].reshape(D_MODEL, A_KV_HEADS, 1, HEAD_DIM), g, axis=2).reshape(D_MODEL, A_Q)
    av = jnp.repeat(w[:, A_Q + A_KV:A_Q + 2 * A_KV].reshape(D_MODEL, A_KV_HEADS, 1, HEAD_DIM), g, axis=2).reshape(D_MODEL, A_Q)
    return jnp.concatenate([aq, ak, av, w[:, A_Q + 2 * A_KV:]], axis=1).astype(BF16)


def _banded_kernel(*refs, tq, koff, wband, halo, has_sink, emit_lse):
    q_ref, kp_ref, kc_ref, kn_ref, vp_ref, vc_ref, vn_ref, bias_ref, hmask_ref = refs[:9]
    rest = refs[9:]
    if has_sink:
        sink_ref, rest = rest[0], rest[1:]
    o_ref = rest[0]
    lse_ref = rest[1] if emit_lse else None

    first = pl.program_id(2) == 0
    last = pl.program_id(2) == pl.num_programs(2) - 1
    col = lax.broadcasted_iota(jnp.int32, (1, wband), 1)
    lane_head = lax.broadcasted_iota(jnp.int32, (BLOCK, HEADS_W), 1) // HEAD_DIM
    if has_sink:
        sink_col = lax.broadcasted_iota(jnp.int32, (1, LANES), 1) == 0

        def in_col0(x, new):
            return jnp.concatenate([jnp.where(sink_col, new, x[:, :LANES]), x[:, LANES:]], axis=1)
    for i in range(tq // BLOCK):
        lo = BLOCK * i - koff
        hi = lo + wband

        def band(p_ref, c_ref, n_ref):
            parts = []
            if lo < 0:
                parts.append(p_ref[halo + lo:halo, :])
            parts.append(c_ref[max(lo, 0):min(hi, tq), :])
            if hi > tq:
                parts.append(n_ref[0:hi - tq, :])
            return parts[0] if len(parts) == 1 else jnp.concatenate(parts, axis=0)

        kb = band(kp_ref, kc_ref, kn_ref)
        vb = band(vp_ref, vc_ref, vn_ref)
        q = q_ref[BLOCK * i:BLOCK * (i + 1), :]
        qs = jnp.concatenate([q] * 4, axis=0) * hmask_ref[...]
        s = _dot_nt(qs, kb) + bias_ref[...]
        if lo < 0:
            s = s + jnp.where(first & (col < -lo), NEG, 0.0)
        if hi > tq:
            s = s + jnp.where(last & (col >= wband - (hi - tq)), NEG, 0.0)
        if has_sink:
            s = in_col0(s, sink_ref[...])
        m = jnp.max(s, axis=-1, keepdims=True)
        p = jnp.exp(s - m)
        l = jnp.sum(p, axis=-1, keepdims=True)
        if has_sink:
            p = in_col0(p, 0.0)
        pv = _dot(p.astype(BF16), vb) / l
        o = jnp.zeros((BLOCK, HEADS_W), F32)
        lse_o = jnp.zeros((BLOCK, HEADS_W), F32)
        lse = m + jnp.log(l) if emit_lse else None
        for h in range(4):
            hr = slice(BLOCK * h, BLOCK * (h + 1))
            o = jnp.where(lane_head == h, pv[hr], o)
            if emit_lse:
                lse_o = jnp.where(lane_head == h, lse[hr], lse_o)
        o_ref[BLOCK * i:BLOCK * (i + 1), :] = o.astype(BF16)
        if emit_lse:
            lse_ref[BLOCK * i:BLOCK * (i + 1), :] = lse_o


def _band_tables(window, koff, wband, slopes_eff):
    il = np.arange(BLOCK)[:, None]
    c = np.arange(wband)[None, :]
    dist = np.abs(c - koff - il).astype(np.float64)
    bias = np.concatenate([np.where(dist <= window, -s * dist, NEG) for s in slopes_eff], axis=0)
    hmask = np.concatenate([np.broadcast_to((np.arange(HEADS_W) // HEAD_DIM == h)[None, :], (BLOCK, HEADS_W))
                            for h in range(4)], axis=0)
    return jnp.asarray(bias, dtype=F32), jnp.asarray(hmask, dtype=BF16)


def _banded_attention(q, k, v, *, window, slopes_eff, sink=None, emit_lse=False):
    B, d, Sd, _ = q.shape
    tq = min(512, Sd)
    koff, wband, halo = (BLOCK // 2, 2 * BLOCK, BLOCK) if window <= BLOCK // 2 else (3 * BLOCK // 2, 4 * BLOCK, 2 * BLOCK)
    halo = min(halo, tq)
    assert koff <= halo and wband - koff - BLOCK <= halo and window <= koff and window <= wband - koff - BLOCK
    nh, nhb = tq // halo, Sd // halo
    bias, hmask = _band_tables(window, koff, wband, slopes_eff)
    cur = pl.BlockSpec((None, None, tq, HEADS_W), lambda b, r, i: (b, r, i, 0))
    prev = pl.BlockSpec((None, None, halo, HEADS_W), lambda b, r, i: (b, r, jnp.maximum(i * nh - 1, 0), 0))
    nxt = pl.BlockSpec((None, None, halo, HEADS_W), lambda b, r, i: (b, r, jnp.minimum((i + 1) * nh, nhb - 1), 0))
    const = lambda b, r, i: (0, 0)
    in_specs = [cur, prev, cur, nxt, prev, cur, nxt,
                pl.BlockSpec((4 * BLOCK, wband), const), pl.BlockSpec((4 * BLOCK, HEADS_W), const)]
    args = [q, k, k, k, v, v, v, bias, hmask]
    if sink is not None:
        assert koff > window
        in_specs.append(pl.BlockSpec((4 * BLOCK, LANES), const))
        args.append(jnp.broadcast_to(jnp.repeat(sink.astype(F32), BLOCK)[:, None], (4 * BLOCK, LANES)))
    out_shape = [jax.ShapeDtypeStruct((B, d, Sd, HEADS_W), BF16)]
    out_specs = [cur]
    if emit_lse:
        out_shape.append(jax.ShapeDtypeStruct((B, d, Sd, HEADS_W), F32))
        out_specs.append(cur)
    return pl.pallas_call(
        functools.partial(_banded_kernel, tq=tq, koff=koff, wband=wband, halo=halo, has_sink=sink is not None,
                          emit_lse=emit_lse),
        grid=(B, d, Sd // tq),
        in_specs=in_specs,
        out_specs=out_specs,
        out_shape=out_shape,
        compiler_params=pltpu.CompilerParams(dimension_semantics=("parallel", "parallel", "parallel"),
                                             vmem_limit_bytes=VMEM_LIMIT_BYTES),
    )(*args)


_SOFTMAX_ROWS = 128


def _diff_kernel(slope_ref, lam_ref, subln_ref, q_ref, k_ref, v_ref, o_ref,
                 qs_ref, m_ref, acc_ref, s0_ref, s1_ref, p0_ref, p1_ref, a0_ref, a1_ref,
                 *, tq, nk, lam_init):
    qi = pl.program_id(2)
    s_refs, p_refs, a_refs = (s0_ref, s1_ref), (p0_ref, p1_ref), (a0_ref, a1_ref)

    q = q_ref[...].astype(F32)
    lane = lax.broadcasted_iota(jnp.int32, (tq, LANES), 1)
    pos = qi * tq + lax.broadcasted_iota(jnp.int32, (tq, LANES), 0)
    slope = slope_ref[...]
    base = jnp.where(lane == 0, -slope * 128.0 * (pos >> 7).astype(F32),
                     jnp.where(lane == 1, -slope * (pos & 127).astype(F32),
                               jnp.where(lane == 2, slope * 128.0,
                                         jnp.where(lane == 3, slope, 0.0))))
    q1 = jnp.where(lane < HEAD_DIM, q, 0.0).astype(BF16)
    q2 = jnp.where(lane >= HEAD_DIM, q, 0.0).astype(BF16)
    for var, sign in enumerate((1.0, -1.0, 0.0)):
        aug = (sign * base).astype(BF16)
        qs_ref[var, 0:tq, 0:LANES] = q1
        qs_ref[var, 0:tq, LANES:2 * LANES] = aug
        qs_ref[var, tq:2 * tq, 0:LANES] = q2
        qs_ref[var, tq:2 * tq, LANES:2 * LANES] = aug
    m_ref[...] = jnp.full(m_ref.shape, NEG, F32)
    acc_ref[...] = jnp.zeros(acc_ref.shape, F32)

    def key_tile(t):
        return jnp.where(t == 0, qi, jnp.where(t <= qi, t - 1, t))

    def rows_of(j):
        return pl.ds(pl.multiple_of(j * tq, tq), tq)

    def stage1_diag(slot):
        s = _dot_nt(qs_ref[2], k_ref[rows_of(qi), :])
        r = lax.broadcasted_iota(jnp.int32, s.shape, 0) & (tq - 1)
        c = lax.broadcasted_iota(jnp.int32, s.shape, 1)
        s_refs[slot][...] = s - slope_ref[:, 0:1] * jnp.abs(r - c).astype(F32)

    def stage1(t, slot):
        j = key_tile(t)
        s_refs[slot][...] = _dot_nt(qs_ref[jnp.where(j < qi, 0, 1)], k_ref[rows_of(j), :])

    def stage2(slot):
        for r0 in range(0, 2 * tq, _SOFTMAX_ROWS):
            rows = slice(r0, r0 + _SOFTMAX_ROWS)
            s = s_refs[slot][rows, :]
            m_prev = m_ref[rows, :]
            m_new = jnp.maximum(m_prev, jnp.max(s, axis=-1, keepdims=True))
            a_refs[slot][rows, :] = jnp.exp(m_prev - m_new)
            p = jnp.exp(s - jnp.concatenate([m_new] * (tq // LANES), axis=1))
            p_refs[slot][rows, :] = p.astype(BF16)
            m_ref[rows, :] = m_new

    def stage3(j, slot):
        pv = _dot(p_refs[slot][...], v_ref[rows_of(j), :])
        a = a_refs[slot][...]
        acc_ref[...] = acc_ref[...] * jnp.concatenate([a, a], axis=1) + pv

    stage1_diag(0)
    stage1(1, 1)
    stage2(0)

    def steady(u, carry):
        t = 2 + 2 * u
        stage1(t, 0)
        stage3(key_tile(t - 2), 0)
        stage2(1)
        stage1(t + 1, 1)
        stage3(key_tile(t - 1), 1)
        stage2(0)
        return carry

    lax.fori_loop(0, (nk - 2) // 2, steady, 0)
    stage3(key_tile(nk - 2), 0)
    stage2(1)
    stage3(key_tile(nk - 1), 1)

    acc = acc_ref[...]
    o1 = acc[:tq, :LANES] / acc[:tq, LANES:]
    o2 = acc[tq:, :LANES] / acc[tq:, LANES:]
    lv = lam_ref[...]
    lam = (jnp.exp(jnp.sum(lv[0:1] * lv[1:2], axis=-1, keepdims=True))
           - jnp.exp(jnp.sum(lv[2:3] * lv[3:4], axis=-1, keepdims=True)) + lam_init)
    o = _rms(o1 - lam * o2, subln_ref[...], SUBLN_EPS) * (1.0 - lam_init)
    o_ref[...] = o.astype(BF16)


def _diff_attention(q, k, v, lam_vecs, subln, *, B, S, lam_init):
    tq = min(512, S // 2)
    nq = S // tq
    assert nq >= 2 and nq % 2 == 0
    slopes = np.asarray(_alibi_slopes(B_HEADS), np.float32)
    slope_tab = jnp.asarray(np.broadcast_to(slopes[:, None, None], (B_HEADS, 1, LANES)).copy())
    qv = q.reshape(B, S, B_QK)
    kv = k.reshape(B_HEADS, B, S, 2 * LANES)
    vv = v.reshape(B_HEADS, B, S, 2 * LANES)
    out = pl.pallas_call(
        functools.partial(_diff_kernel, tq=tq, nk=nq, lam_init=lam_init),
        grid=(B, B_HEADS, nq),
        in_specs=[pl.BlockSpec((None, 1, LANES), lambda b, h, i: (h, 0, 0)),
                  pl.BlockSpec((4, HEAD_DIM), lambda b, h, i: (0, 0)),
                  pl.BlockSpec((1, B_VDIM), lambda b, h, i: (0, 0)),
                  pl.BlockSpec((None, tq, LANES), lambda b, h, i: (b, i, h)),
                  pl.BlockSpec((None, None, S, 2 * LANES), lambda b, h, i: (h, b, 0, 0)),
                  pl.BlockSpec((None, None, S, 2 * LANES), lambda b, h, i: (h, b, 0, 0))],
        out_specs=pl.BlockSpec((None, tq, LANES), lambda b, h, i: (b, i, h)),
        out_shape=jax.ShapeDtypeStruct((B, S, B_V), BF16),
        scratch_shapes=[pltpu.VMEM((3, 2 * tq, 2 * LANES), BF16),
                        pltpu.VMEM((2 * tq, LANES), F32),
                        pltpu.VMEM((2 * tq, 2 * LANES), F32),
                        pltpu.VMEM((2 * tq, tq), F32), pltpu.VMEM((2 * tq, tq), F32),
                        pltpu.VMEM((2 * tq, tq), BF16), pltpu.VMEM((2 * tq, tq), BF16),
                        pltpu.VMEM((2 * tq, LANES), F32), pltpu.VMEM((2 * tq, LANES), F32)],
        compiler_params=pltpu.CompilerParams(
            dimension_semantics=("parallel", "parallel", "arbitrary"),
            vmem_limit_bytes=VMEM_LIMIT_BYTES),
    )(slope_tab, lam_vecs.astype(F32), subln.reshape(1, B_VDIM).astype(F32), qv, kv, vv)
    return out.reshape(B * S, B_V)


def _mixout_kernel(x_ref, oa_ref, ob_ref, g0_ref, g1_ref, g2_ref, l0_ref, l1_ref, l2_ref, w_ref, ln_ref,
                   xo_ref, h_ref, *slabs, tm):
    slabs = list(slabs)

    def tokens(ref, d):
        if d == 1:
            return ref[0].astype(F32)
        halves = []
        for c in range(HEADS_W // LANES):
            buf = slabs.pop()
            for r in range(d):
                buf[pl.ds(r, tm // d, stride=d), :] = ref[r, :, c * LANES:(c + 1) * LANES].astype(F32)
            halves.append(buf[...])
        return jnp.concatenate(halves, axis=1)

    dil = [d for _, d in C_PAIRS]
    l0, l1, l2 = tokens(l0_ref, dil[0]), tokens(l1_ref, dil[1]), tokens(l2_ref, dil[2])
    g0, g1, g2 = tokens(g0_ref, dil[0]), tokens(g1_ref, dil[1]), tokens(g2_ref, dil[2])
    mx = jnp.maximum(jnp.maximum(l0, l1), l2)
    e0, e1, e2 = jnp.exp(l0 - mx), jnp.exp(l1 - mx), jnp.exp(l2 - mx)
    oc = (e0 * g0 + e1 * g1 + e2 * g2) / (e0 + e1 + e2)
    mix =(_dot(oa_ref[...], w_ref[0:A_Q, :])
           + _dot(ob_ref[...], w_ref[A_Q:A_Q + B_V, :])
           + _dot(oc.astype(BF16), w_ref[A_Q + B_V:, :]))
    x = x_ref[...] + mix
    xo_ref[...] = x
    h_ref[...] = _rms(x, ln_ref[...], EPS).astype(BF16)


def _mix_out(x, oa, ob, og, lse, w_out, ln2, B, S):
    T = x.shape[0]
    tm = min(512, S)
    nt = S // tm
    row = lambda i: (i, 0)
    const = lambda i: (0, 0)
    grp = [pl.BlockSpec((None, d, tm // d, HEADS_W), lambda i: (i // nt, 0, i % nt, 0)) for _, d in C_PAIRS]
    n_slabs = 2 * sum(1 for _, d in C_PAIRS if d > 1)
    return pl.pallas_call(
        functools.partial(_mixout_kernel, tm=tm),
        grid=(T // tm,),
        in_specs=[pl.BlockSpec((tm, D_MODEL), row),
                  pl.BlockSpec((tm, A_Q), row),
                  pl.BlockSpec((tm, B_V), row)]
                 + grp + grp
                 + [pl.BlockSpec((D_MODEL, D_MODEL), const),
                    pl.BlockSpec((1, D_MODEL), const)],
        out_specs=[pl.BlockSpec((tm, D_MODEL), row), pl.BlockSpec((tm, D_MODEL), row)],
        out_shape=[jax.ShapeDtypeStruct((T, D_MODEL), F32), jax.ShapeDtypeStruct((T, D_MODEL), BF16)],
        scratch_shapes=[pltpu.VMEM((tm, LANES), F32)] * (n_slabs * (HEADS_W // LANES)),
        compiler_params=pltpu.CompilerParams(dimension_semantics=("parallel",),
                                             vmem_limit_bytes=VMEM_LIMIT_BYTES),
    )(x, oa, ob, *og, *lse, w_out, ln2.reshape(1, D_MODEL))


def _matmul_kernel(a_ref, w_ref, o_ref):
    o_ref[...] = _dot(a_ref[...], w_ref[...]).astype(o_ref.dtype)


def _matmul(a, w, tn):
    T, K = a.shape
    N = w.shape[1]
    tm = min(1024, T)
    return pl.pallas_call(
        _matmul_kernel,
        grid=(T // tm, N // tn),
        in_specs=[pl.BlockSpec((tm, K), lambda i, j: (i, 0)),
                  pl.BlockSpec((K, tn), lambda i, j: (0, j))],
        out_specs=pl.BlockSpec((tm, tn), lambda i, j: (i, j)),
        out_shape=jax.ShapeDtypeStruct((T, N), BF16),
        compiler_params=pltpu.CompilerParams(dimension_semantics=("parallel", "parallel"),
                                             vmem_limit_bytes=VMEM_LIMIT_BYTES),
    )(a, w)


_CONV_CHUNK = 256


def _convdown_kernel(u_ref, up_ref, un_ref, cw_ref, cb_ref, wd_ref, x_ref, lnf_ref, o_ref, act_ref,
                     *, tm, seq_len, final_norm):
    pos0 = (pl.program_id(0) * tm) % seq_len
    keep_prev = jnp.where(pos0 == 0, 0.0, 1.0)
    keep_next = jnp.where(pos0 + tm == seq_len, 0.0, 1.0)
    row = lax.broadcasted_iota(jnp.int32, (tm, _CONV_CHUNK), 0)

    def conv(c0):
        cols = slice(c0, c0 + _CONV_CHUNK)
        uc = u_ref[:, cols].astype(F32)
        before = up_ref[SUBLANES - 1:SUBLANES, cols].astype(F32) * keep_prev
        after = un_ref[0:1, cols].astype(F32) * keep_next
        u_prev = jnp.where(row == 0, before, pltpu.roll(uc, 1, axis=0))
        u_next = jnp.where(row == tm - 1, after, pltpu.roll(uc, tm - 1, axis=0))
        return (u_prev * cw_ref[0:1, cols] + uc * cw_ref[1:2, cols] + u_next * cw_ref[2:3, cols]
                + cb_ref[:, cols])

    for c0 in range(0, D_FF, _CONV_CHUNK):
        gate = conv(c0)
        val = conv(D_FF + c0)
        act_ref[:, c0:c0 + _CONV_CHUNK] = (gate * jax.nn.sigmoid(gate) * val).astype(BF16)
    x = x_ref[...] + _dot(act_ref[...], wd_ref[...])
    if final_norm:
        x = _rms(x, lnf_ref[...], EPS)
    o_ref[...] = x


def _conv_down(u, conv_w, conv_b, w_down, x, ln_f, *, seq_len, final_norm):
    T = x.shape[0]
    tm = min(512, T)
    nh = tm // SUBLANES
    n8 = T // SUBLANES
    row = lambda i: (i, 0)
    const = lambda i: (0, 0)
    return pl.pallas_call(
        functools.partial(_convdown_kernel, tm=tm, seq_len=seq_len, final_norm=final_norm),
        grid=(T // tm,),
        in_specs=[pl.BlockSpec((tm, 2 * D_FF), row),
                  pl.BlockSpec((SUBLANES, 2 * D_FF), lambda i: (jnp.maximum(i * nh - 1, 0), 0)),
                  pl.BlockSpec((SUBLANES, 2 * D_FF), lambda i: (jnp.minimum((i + 1) * nh, n8 - 1), 0)),
                  pl.BlockSpec((3, 2 * D_FF), const),
                  pl.BlockSpec((1, 2 * D_FF), const),
                  pl.BlockSpec((D_FF, D_MODEL), const),
                  pl.BlockSpec((tm, D_MODEL), row),
                  pl.BlockSpec((1, D_MODEL), const)],
        out_specs=pl.BlockSpec((tm, D_MODEL), row),
        out_shape=jax.ShapeDtypeStruct((T, D_MODEL), F32),
        scratch_shapes=[pltpu.VMEM((tm, D_FF), BF16)],
        compiler_params=pltpu.CompilerParams(dimension_semantics=("parallel",),
                                             vmem_limit_bytes=VMEM_LIMIT_BYTES),
    )(u, u, u, conv_w, conv_b.reshape(1, 2 * D_FF), w_down, x, ln_f.reshape(1, D_MODEL))


def _trunk(x, p):
    B, S = x.shape[0], x.shape[1]
    xf = x.reshape(B * S, D_MODEL)
    a_slopes = _alibi_slopes(A_HEADS)
    c_slopes = _alibi_slopes(C_GROUPS * C_HEADS)
    for l in range(DEPTH):
        aq, ak, av, bq, bk, bv, *c = _in_proj(xf, p["ln1"][l], p["w_in"][l], B, S)
        as_seq = lambda t: t.reshape(B, 1, S, HEADS_W)
        (oa,) = _banded_attention(as_seq(aq), as_seq(ak), as_seq(av), window=A_WINDOW, slopes_eff=a_slopes,
                                  sink=p["a_sink"][l])
        lam_init = 0.8 - 0.6 * math.exp(-0.3 * l)
        ob = _diff_attention(bq, bk, bv, p["lam"][l], p["subln"][l], B=B, S=S, lam_init=lam_init)
        og, lse = [], []
        for g, (w, d) in enumerate(C_PAIRS):
            o_g, lse_g = _banded_attention(
                c[g], c[3 + g], c[6 + g], window=w // (2 * d),
                slopes_eff=[s * d for s in c_slopes[g * C_HEADS:(g + 1) * C_HEADS]], emit_lse=True)
            og.append(o_g)
            lse.append(lse_g)
        xf, h2 = _mix_out(xf, oa.reshape(B * S, HEADS_W), ob, og, lse, p["w_out"][l], p["ln2"][l], B, S)
        u = _matmul(h2, p["w_up"][l], D_FF)
        xf = _conv_down(u, p["conv_w"][l], p["conv_b"][l], p["w_down"][l], xf, p["ln_f"],
                        seq_len=S, final_norm=(l == DEPTH - 1))
    return xf.reshape(B, S, D_MODEL)


def kernel(x_prompt, x_sample, ln1, w_in, a_sink, lam_q1, lam_k1, lam_q2, lam_k2, subln, w_out, ln2, w_up,
           conv_w, conv_b, w_down, ln_f):
    p = {
        "ln1": ln1,
        "w_in": [_prep_w_in(w_in[l]) for l in range(DEPTH)],
        "a_sink": a_sink,
        "lam": jnp.stack([lam_q1, lam_k1, lam_q2, lam_k2], axis=1),
        "subln": subln,
        "w_out": w_out.astype(BF16),
        "ln2": ln2,
        "w_up": w_up.astype(BF16),
        "conv_w": conv_w,
        "conv_b": conv_b,
        "w_down": w_down.astype(BF16),
        "ln_f": ln_f,
    }
    return (_trunk(x_prompt, p), _trunk(x_sample, p))
```

```python
import functools
import math

import numpy as np
import jax
import jax.numpy as jnp
from jax import lax
from jax.experimental import pallas as pl
from jax.experimental.pallas import tpu as pltpu

F32 = jnp.float32
BF16 = jnp.bfloat16

D_MODEL = 1024
DEPTH = 2
HEAD_DIM = 64
BLOCK = 128
EPS = 1e-6
SUBLN_EPS = 1e-5
NEG = -1e30
SCALE = HEAD_DIM ** -0.5
A_HEADS = 4
A_KV_HEADS = 2
A_WINDOW = 128
B_HEADS = 4
B_VDIM = 2 * HEAD_DIM
C_PAIRS = ((128, 1), (512, 4), (2048, 16))
C_GROUPS = 3
C_HEADS = 4
D_FF = 2816
A_Q = A_HEADS * HEAD_DIM
A_KV = A_KV_HEADS * HEAD_DIM
B_QK = B_HEADS * 2 * HEAD_DIM
B_V = B_HEADS * B_VDIM
C_QKV = C_GROUPS * C_HEADS * HEAD_DIM
HEADS_W = 4 * HEAD_DIM

VMEM_LIMIT_BYTES = 56 * 1024 * 1024
LANES = 128
SUBLANES = 8


def _alibi_slopes(n):
    return [2.0 ** (-8.0 * k / n) for k in range(1, n + 1)]


LOG2E = math.log2(math.e)
_AUG_SPLITS = 3


def _bf16_round(x):
    b = np.asarray(x, np.float32).view(np.uint32)
    return ((b + (((b >> 16) & 1) + 0x7FFF)) & 0xFFFF0000).astype(np.uint32).view(np.float32)


def _diff_alibi_tables():
    c = np.asarray(_alibi_slopes(B_HEADS), np.float64) * LOG2E
    pieces, rem = [], c.copy()
    for _ in range(_AUG_SPLITS):
        pk = _bf16_round(rem).astype(np.float64)
        pieces.append(pk)
        rem = rem - pk
    pieces = np.stack(pieces, axis=1)
    ktab = np.zeros((B_HEADS, LANES), np.float32)
    qtab = np.zeros((B_HEADS, 1, LANES), np.float32)
    ktab[:, 0:3] = pieces
    ktab[:, 3:6] = pieces
    qtab[:, 0, 6:9] = pieces
    qtab[:, 0, 9:12] = pieces
    cfull = np.broadcast_to(c.astype(np.float32)[:, None, None], (B_HEADS, 1, LANES)).copy()
    return jnp.asarray(ktab), jnp.asarray(qtab), jnp.asarray(cfull)


def _dot(a, b):
    return jnp.dot(a, b, preferred_element_type=F32)


def _dot_nt(a, b):
    return lax.dot_general(a, b, (((1,), (1,)), ((), ())), preferred_element_type=F32)


def _rms(x, g, eps):
    return x * lax.rsqrt(jnp.mean(x * x, axis=-1, keepdims=True) + eps) * g


_W_AQ, _W_AK, _W_AV = 0, 256, 512
_W_BQ, _W_BK, _W_BV = 768, 1280, 1792
_W_CQ, _W_CK, _W_CV = 2304, 3072, 3840
_W_COLS = 4608


def _inproj_kernel(x_ref, g_ref, w_ref, ktab_ref, aq_ref, ak_ref, av_ref, bq_ref, bk_ref, bv_ref, *rest, seq_len, tm):
    c_refs, slabs = rest[:9], rest[9:]
    h = _rms(x_ref[...], g_ref[...], EPS).astype(BF16)

    def proj(c0, c1):
        return _dot(h, w_ref[:, c0:c1])

    aq_ref[...] = (proj(_W_AQ, _W_AK) * SCALE).astype(BF16)
    ak_ref[...] = proj(_W_AK, _W_AV).astype(BF16)
    av_ref[...] = proj(_W_AV, _W_BQ).astype(BF16)
    bq_ref[...] = (proj(_W_BQ, _W_BK) * (SCALE * LOG2E)).astype(BF16)
    pos = (pl.program_id(0) * tm + lax.broadcasted_iota(jnp.int32, (tm, LANES), 0)) % seq_len
    lane = lax.broadcasted_iota(jnp.int32, (tm, LANES), 1)
    pos_cols = jnp.where(lane < 9, ((pos >> 7) * 128).astype(F32),
                         jnp.where(lane < 12, (pos & 127).astype(F32), 0.0))
    ones = jnp.ones((tm, LANES), BF16)
    bk = proj(_W_BK, _W_BV).astype(BF16)
    bv = proj(_W_BV, _W_CQ).astype(BF16)
    for hh in range(B_HEADS):
        bk_ref[hh, :, 0:LANES] = bk[:, hh * LANES:(hh + 1) * LANES]
        bk_ref[hh, :, LANES:2 * LANES] = jnp.where(lane < 6, ktab_ref[hh:hh + 1, :], pos_cols).astype(BF16)
        bv_ref[hh, :, 0:LANES] = bv[:, hh * LANES:(hh + 1) * LANES]
        bv_ref[hh, :, LANES:2 * LANES] = ones
    slab = 0
    for part, (c0, scale) in enumerate(((_W_CQ, SCALE), (_W_CK, 1.0), (_W_CV, 1.0))):
        pc = proj(c0, c0 + C_QKV) * scale
        for g, (_, d) in enumerate(C_PAIRS):
            dst = c_refs[3 * part + g]
            val = pc[:, g * HEADS_W:(g + 1) * HEADS_W]
            if d == 1:
                dst[0] = val.astype(BF16)
            else:
                for c in range(HEADS_W // LANES):
                    buf = slabs[slab]
                    slab += 1
                    lanes = slice(c * LANES, (c + 1) * LANES)
                    buf[...] = val[:, lanes]
                    for r in range(d):
                        dst[r, :, lanes] = buf[pl.ds(r, tm // d, stride=d), :].astype(BF16)


def _in_proj(x, g, w, B, S):
    T = x.shape[0]
    tm = min(512, S)
    nt = S // tm
    row = lambda i: (i, 0)
    const = lambda i: (0, 0)
    outs = [
        (jax.ShapeDtypeStruct((T, 256), BF16), pl.BlockSpec((tm, 256), row)),
        (jax.ShapeDtypeStruct((T, 256), BF16), pl.BlockSpec((tm, 256), row)),
        (jax.ShapeDtypeStruct((T, 256), BF16), pl.BlockSpec((tm, 256), row)),
        (jax.ShapeDtypeStruct((T, B_QK), BF16), pl.BlockSpec((tm, B_QK), row)),
        (jax.ShapeDtypeStruct((B_HEADS, T, 256), BF16), pl.BlockSpec((B_HEADS, tm, 256), lambda i: (0, i, 0))),
        (jax.ShapeDtypeStruct((B_HEADS, T, 256), BF16), pl.BlockSpec((B_HEADS, tm, 256), lambda i: (0, i, 0))),
    ]
    for _ in range(3):
        for _, d in C_PAIRS:
            outs.append((jax.ShapeDtypeStruct((B, d, S // d, HEADS_W), BF16),
                         pl.BlockSpec((None, d, tm // d, HEADS_W), lambda i: (i // nt, 0, i % nt, 0))))
    n_slabs = 3 * sum(1 for _, d in C_PAIRS if d > 1)
    return pl.pallas_call(
        functools.partial(_inproj_kernel, seq_len=S, tm=tm),
        grid=(T // tm,),
        in_specs=[pl.BlockSpec((tm, D_MODEL), row),
                  pl.BlockSpec((1, D_MODEL), const),
                  pl.BlockSpec((D_MODEL, _W_COLS), const),
                  pl.BlockSpec((B_HEADS, LANES), const)],
        out_specs=[o[1] for o in outs],
        out_shape=[o[0] for o in outs],
        scratch_shapes=[pltpu.VMEM((tm, LANES), F32)] * (n_slabs * (HEADS_W // LANES)),
        compiler_params=pltpu.CompilerParams(dimension_semantics=("parallel",),
                                             vmem_limit_bytes=VMEM_LIMIT_BYTES),
    )(x, g.reshape(1, D_MODEL), w, _diff_alibi_tables()[0])


def _prep_w_in(w):
    g = A_HEADS // A_KV_HEADS
    aq = w[:, :A_Q]
    ak = jnp.repeat(w[:, A_Q:A_Q + A_KV].reshape(D_MODEL, A_KV_HEADS, 1, HEAD_DIM), g, axis=2).reshape(D_MODEL, A_Q)
    av = jnp.repeat(w[:, A_Q + A_KV:A_Q + 2 * A_KV].reshape(D_MODEL, A_KV_HEADS, 1, HEAD_DIM), g, axis=2).reshape(D_MODEL, A_Q)
    return jnp.concatenate([aq, ak, av, w[:, A_Q + 2 * A_KV:]], axis=1).astype(BF16)


def _banded_kernel(*refs, tq, koff, wband, halo, has_sink, emit_lse):
    q_ref, kp_ref, kc_ref, kn_ref, vp_ref, vc_ref, vn_ref, bias_ref, hmask_ref = refs[:9]
    rest = refs[9:]
    if has_sink:
        sink_ref, rest = rest[0], rest[1:]
    o_ref = rest[0]
    lse_ref = rest[1] if emit_lse else None

    first = pl.program_id(2) == 0
    last = pl.program_id(2) == pl.num_programs(2) - 1
    col = lax.broadcasted_iota(jnp.int32, (1, wband), 1)
    lane_head = lax.broadcasted_iota(jnp.int32, (BLOCK, HEADS_W), 1) // HEAD_DIM
    if has_sink:
        sink_col = lax.broadcasted_iota(jnp.int32, (1, LANES), 1) == 0

        def in_col0(x, new):
            return jnp.concatenate([jnp.where(sink_col, new, x[:, :LANES]), x[:, LANES:]], axis=1)
    for i in range(tq // BLOCK):
        lo = BLOCK * i - koff
        hi = lo + wband

        def band(p_ref, c_ref, n_ref):
            parts = []
            if lo < 0:
                parts.append(p_ref[halo + lo:halo, :])
            parts.append(c_ref[max(lo, 0):min(hi, tq), :])
            if hi > tq:
                parts.append(n_ref[0:hi - tq, :])
            return parts[0] if len(parts) == 1 else jnp.concatenate(parts, axis=0)

        kb = band(kp_ref, kc_ref, kn_ref)
        vb = band(vp_ref, vc_ref, vn_ref)
        q = q_ref[BLOCK * i:BLOCK * (i + 1), :]
        qs = jnp.concatenate([q] * 4, axis=0) * hmask_ref[...]
        s = _dot_nt(qs, kb) + bias_ref[...]
        if lo < 0:
            s = s + jnp.where(first & (col < -lo), NEG, 0.0)
        if hi > tq:
            s = s + jnp.where(last & (col >= wband - (hi - tq)), NEG, 0.0)
        if has_sink:
            s = in_col0(s, sink_ref[...])
        m = jnp.max(s, axis=-1, keepdims=True)
        p = jnp.exp(s - m)
        l = jnp.sum(p, axis=-1, keepdims=True)
        if has_sink:
            p = in_col0(p, 0.0)
        pv = _dot(p.astype(BF16), vb) / l
        o = jnp.zeros((BLOCK, HEADS_W), F32)
        lse_o = jnp.zeros((BLOCK, HEADS_W), F32)
        lse = m + jnp.log(l) if emit_lse else None
        for h in range(4):
            hr = slice(BLOCK * h, BLOCK * (h + 1))
            o = jnp.where(lane_head == h, pv[hr], o)
            if emit_lse:
                lse_o = jnp.where(lane_head == h, lse[hr], lse_o)
        o_ref[BLOCK * i:BLOCK * (i + 1), :] = o.astype(BF16)
        if emit_lse:
            lse_ref[BLOCK * i:BLOCK * (i + 1), :] = lse_o


def _band_tables(window, koff, wband, slopes_eff):
    il = np.arange(BLOCK)[:, None]
    c = np.arange(wband)[None, :]
    dist = np.abs(c - koff - il).astype(np.float64)
    bias = np.concatenate([np.where(dist <= window, -s * dist, NEG) for s in slopes_eff], axis=0)
    hmask = np.concatenate([np.broadcast_to((np.arange(HEADS_W) // HEAD_DIM == h)[None, :], (BLOCK, HEADS_W))
                            for h in range(4)], axis=0)
    return jnp.asarray(bias, dtype=F32), jnp.asarray(hmask, dtype=BF16)


def _banded_attention(q, k, v, *, window, slopes_eff, sink=None, emit_lse=False):
    B, d, Sd, _ = q.shape
    tq = min(512, Sd)
    koff, wband, halo = (BLOCK // 2, 2 * BLOCK, BLOCK) if window <= BLOCK // 2 else (3 * BLOCK // 2, 4 * BLOCK, 2 * BLOCK)
    halo = min(halo, tq)
    assert koff <= halo and wband - koff - BLOCK <= halo and window <= koff and window <= wband - koff - BLOCK
    nh, nhb = tq // halo, Sd // halo
    bias, hmask = _band_tables(window, koff, wband, slopes_eff)
    cur = pl.BlockSpec((None, None, tq, HEADS_W), lambda b, r, i: (b, r, i, 0))
    prev = pl.BlockSpec((None, None, halo, HEADS_W), lambda b, r, i: (b, r, jnp.maximum(i * nh - 1, 0), 0))
    nxt = pl.BlockSpec((None, None, halo, HEADS_W), lambda b, r, i: (b, r, jnp.minimum((i + 1) * nh, nhb - 1), 0))
    const = lambda b, r, i: (0, 0)
    in_specs = [cur, prev, cur, nxt, prev, cur, nxt,
                pl.BlockSpec((4 * BLOCK, wband), const), pl.BlockSpec((4 * BLOCK, HEADS_W), const)]
    args = [q, k, k, k, v, v, v, bias, hmask]
    if sink is not None:
        assert koff > window
        in_specs.append(pl.BlockSpec((4 * BLOCK, LANES), const))
        args.append(jnp.broadcast_to(jnp.repeat(sink.astype(F32), BLOCK)[:, None], (4 * BLOCK, LANES)))
    out_shape = [jax.ShapeDtypeStruct((B, d, Sd, HEADS_W), BF16)]
    out_specs = [cur]
    if emit_lse:
        out_shape.append(jax.ShapeDtypeStruct((B, d, Sd, HEADS_W), F32))
        out_specs.append(cur)
    return pl.pallas_call(
        functools.partial(_banded_kernel, tq=tq, koff=koff, wband=wband, halo=halo, has_sink=sink is not None,
                          emit_lse=emit_lse),
        grid=(B, d, Sd // tq),
        in_specs=in_specs,
        out_specs=out_specs,
        out_shape=out_shape,
        compiler_params=pltpu.CompilerParams(dimension_semantics=("parallel", "parallel", "parallel"),
                                             vmem_limit_bytes=VMEM_LIMIT_BYTES),
    )(*args)


_SOFTMAX_ROWS = 128
_PIPE_SLOTS = 4


def _diff_kernel(coef_ref, qtab_ref, lam_ref, subln_ref, q_ref, k_ref, v_ref, o_ref,
                 qs_ref, m_ref, acc_ref, *bufs, tq, nk, lam_init):
    qi = pl.program_id(2)
    s_refs, p_refs, a_refs = bufs[0:4], bufs[4:8], bufs[8:12]

    q = q_ref[...].astype(F32)
    lane = lax.broadcasted_iota(jnp.int32, (tq, LANES), 1)
    pos = qi * tq + lax.broadcasted_iota(jnp.int32, (tq, LANES), 0)
    base = jnp.where(lane < 3, -((pos >> 7) * 128).astype(F32),
                     jnp.where(lane < 6, -(pos & 127).astype(F32), qtab_ref[...]))
    q1 = jnp.where(lane < HEAD_DIM, q, 0.0).astype(BF16)
    q2 = jnp.where(lane >= HEAD_DIM, q, 0.0).astype(BF16)
    for var, sign in enumerate((1.0, -1.0, 0.0)):
        aug = (sign * base).astype(BF16)
        qs_ref[var, 0:tq, 0:LANES] = q1
        qs_ref[var, 0:tq, LANES:2 * LANES] = aug
        qs_ref[var, tq:2 * tq, 0:LANES] = q2
        qs_ref[var, tq:2 * tq, LANES:2 * LANES] = aug
    m_ref[...] = jnp.full(m_ref.shape, NEG, F32)
    acc_ref[...] = jnp.zeros(acc_ref.shape, F32)

    def key_tile(t):
        return jnp.where(t == 0, qi, jnp.where(t <= qi, t - 1, t))

    def rows_of(j):
        return pl.ds(pl.multiple_of(j * tq, tq), tq)

    def stage1_diag(slot):
        s = _dot_nt(qs_ref[2], k_ref[rows_of(qi), :])
        r = lax.broadcasted_iota(jnp.int32, s.shape, 0) & (tq - 1)
        c = lax.broadcasted_iota(jnp.int32, s.shape, 1)
        s_refs[slot][...] = s - coef_ref[:, 0:1] * jnp.abs(r - c).astype(F32)

    def stage1(t, slot):
        j = key_tile(t)
        s_refs[slot][...] = _dot_nt(qs_ref[jnp.where(j < qi, 0, 1)], k_ref[rows_of(j), :])

    def stage2(slot):
        for r0 in range(0, 2 * tq, _SOFTMAX_ROWS):
            rows = slice(r0, r0 + _SOFTMAX_ROWS)
            m_prev = m_ref[rows, :]
            m_new = jnp.maximum(m_prev, jnp.max(s_refs[slot][rows, :], axis=-1, keepdims=True))
            a_refs[slot][rows, :] = jnp.exp2(m_prev - m_new)
            m_ref[rows, :] = m_new
        for r0 in range(0, 2 * tq, _SOFTMAX_ROWS):
            rows = slice(r0, r0 + _SOFTMAX_ROWS)
            m_new = jnp.concatenate([m_ref[rows, :]] * (tq // LANES), axis=1)
            p_refs[slot][rows, :] = jnp.exp2(s_refs[slot][rows, :] - m_new).astype(BF16)

    def stage3(j, slot):
        pv = _dot(p_refs[slot][...], v_ref[rows_of(j), :])
        a = a_refs[slot][...]
        acc_ref[...] = acc_ref[...] * jnp.concatenate([a, a], axis=1) + pv

    def step(u, first, has1=True, has2=True, has3=True):
        other = 2 - first
        if has2:
            stage2(other)
            stage2(other + 1)
        if has1:
            stage1(2 * u, first)
            stage1(2 * u + 1, first + 1)
        if has3:
            stage3(key_tile(2 * u - 4), first)
            stage3(key_tile(2 * u - 3), first + 1)

    stage1_diag(0)
    stage1(1, 1)
    step(1, 2, has3=False)

    for u in range(2, nk // 2):
        step(u, 2 * (u % 2))
    step(nk // 2, 0, has1=False)
    step(nk // 2 + 1, 2, has1=False, has2=False)

    acc = acc_ref[...]
    o1 = acc[:tq, :LANES] / acc[:tq, LANES:]
    o2 = acc[tq:, :LANES] / acc[tq:, LANES:]
    lv = lam_ref[...]
    lam = (jnp.exp(jnp.sum(lv[0:1] * lv[1:2], axis=-1, keepdims=True))
           - jnp.exp(jnp.sum(lv[2:3] * lv[3:4], axis=-1, keepdims=True)) + lam_init)
    o = _rms(o1 - lam * o2, subln_ref[...], SUBLN_EPS) * (1.0 - lam_init)
    o_ref[...] = o.astype(BF16)


def _diff_attention(q, k, v, lam_vecs, subln, *, B, S, lam_init):
    tq = min(512, S // 4)
    nq = S // tq
    assert nq % 4 == 0
    _, qtab, coef = _diff_alibi_tables()
    qv = q.reshape(B, S, B_QK)
    kv = k.reshape(B_HEADS, B, S, 2 * LANES)
    vv = v.reshape(B_HEADS, B, S, 2 * LANES)
    out = pl.pallas_call(
        functools.partial(_diff_kernel, tq=tq, nk=nq, lam_init=lam_init),
        grid=(B, B_HEADS, nq),
        in_specs=[pl.BlockSpec((None, 1, LANES), lambda b, h, i: (h, 0, 0)),
                  pl.BlockSpec((None, 1, LANES), lambda b, h, i: (h, 0, 0)),
                  pl.BlockSpec((4, HEAD_DIM), lambda b, h, i: (0, 0)),
                  pl.BlockSpec((1, B_VDIM), lambda b, h, i: (0, 0)),
                  pl.BlockSpec((None, tq, LANES), lambda b, h, i: (b, i, h)),
                  pl.BlockSpec((None, None, S, 2 * LANES), lambda b, h, i: (h, b, 0, 0)),
                  pl.BlockSpec((None, None, S, 2 * LANES), lambda b, h, i: (h, b, 0, 0))],
        out_specs=pl.BlockSpec((None, tq, LANES), lambda b, h, i: (b, i, h)),
        out_shape=jax.ShapeDtypeStruct((B, S, B_V), BF16),
        scratch_shapes=[pltpu.VMEM((3, 2 * tq, 2 * LANES), BF16),
                        pltpu.VMEM((2 * tq, LANES), F32),
                        pltpu.VMEM((2 * tq, 2 * LANES), F32),
                        *[pltpu.VMEM((2 * tq, tq), F32)] * _PIPE_SLOTS,
                        *[pltpu.VMEM((2 * tq, tq), BF16)] * _PIPE_SLOTS,
                        *[pltpu.VMEM((2 * tq, LANES), F32)] * _PIPE_SLOTS],
        compiler_params=pltpu.CompilerParams(
            dimension_semantics=("parallel", "parallel", "arbitrary"),
            vmem_limit_bytes=VMEM_LIMIT_BYTES),
    )(coef, qtab, lam_vecs.astype(F32), subln.reshape(1, B_VDIM).astype(F32), qv, kv, vv)
    return out.reshape(B * S, B_V)


def _mixout_kernel(x_ref, oa_ref, ob_ref, g0_ref, g1_ref, g2_ref, l0_ref, l1_ref, l2_ref, w_ref, ln_ref,
                   xo_ref, h_ref, *slabs, tm):
    slabs = list(slabs)

    def tokens(ref, d):
        if d == 1:
            return ref[0].astype(F32)
        halves = []
        for c in range(HEADS_W // LANES):
            buf = slabs.pop()
            for r in range(d):
                buf[pl.ds(r, tm // d, stride=d), :] = ref[r, :, c * LANES:(c + 1) * LANES].astype(F32)
            halves.append(buf[...])
        return jnp.concatenate(halves, axis=1)

    dil = [d for _, d in C_PAIRS]
    l0, l1, l2 = tokens(l0_ref, dil[0]), tokens(l1_ref, dil[1]), tokens(l2_ref, dil[2])
    g0, g1, g2 = tokens(g0_ref, dil[0]), tokens(g1_ref, dil[1]), tokens(g2_ref, dil[2])
    mx = jnp.maximum(jnp.maximum(l0, l1), l2)
    e0, e1, e2 = jnp.exp(l0 - mx), jnp.exp(l1 - mx), jnp.exp(l2 - mx)
    oc = (e0 * g0 + e1 * g1 + e2 * g2) / (e0 + e1 + e2)
    mix =(_dot(oa_ref[...], w_ref[0:A_Q, :])
           + _dot(ob_ref[...], w_ref[A_Q:A_Q + B_V, :])
           + _dot(oc.astype(BF16), w_ref[A_Q + B_V:, :]))
    x = x_ref[...] + mix
    xo_ref[...] = x
    h_ref[...] = _rms(x, ln_ref[...], EPS).astype(BF16)


def _mix_out(x, oa, ob, og, lse, w_out, ln2, B, S):
    T = x.shape[0]
    tm = min(512, S)
    nt = S // tm
    row = lambda i: (i, 0)
    const = lambda i: (0, 0)
    grp = [pl.BlockSpec((None, d, tm // d, HEADS_W), lambda i: (i // nt, 0, i % nt, 0)) for _, d in C_PAIRS]
    n_slabs = 2 * sum(1 for _, d in C_PAIRS if d > 1)
    return pl.pallas_call(
        functools.partial(_mixout_kernel, tm=tm),
        grid=(T // tm,),
        in_specs=[pl.BlockSpec((tm, D_MODEL), row),
                  pl.BlockSpec((tm, A_Q), row),
                  pl.BlockSpec((tm, B_V), row)]
                 + grp + grp
                 + [pl.BlockSpec((D_MODEL, D_MODEL), const),
                    pl.BlockSpec((1, D_MODEL), const)],
        out_specs=[pl.BlockSpec((tm, D_MODEL), row), pl.BlockSpec((tm, D_MODEL), row)],
        out_shape=[jax.ShapeDtypeStruct((T, D_MODEL), F32), jax.ShapeDtypeStruct((T, D_MODEL), BF16)],
        scratch_shapes=[pltpu.VMEM((tm, LANES), F32)] * (n_slabs * (HEADS_W // LANES)),
        compiler_params=pltpu.CompilerParams(dimension_semantics=("parallel",),
                                             vmem_limit_bytes=VMEM_LIMIT_BYTES),
    )(x, oa, ob, *og, *lse, w_out, ln2.reshape(1, D_MODEL))


def _matmul_kernel(a_ref, w_ref, o_ref):
    o_ref[...] = _dot(a_ref[...], w_ref[...]).astype(o_ref.dtype)


def _matmul(a, w, tn):
    T, K = a.shape
    N = w.shape[1]
    tm = min(1024, T)
    return pl.pallas_call(
        _matmul_kernel,
        grid=(T // tm, N // tn),
        in_specs=[pl.BlockSpec((tm, K), lambda i, j: (i, 0)),
                  pl.BlockSpec((K, tn), lambda i, j: (0, j))],
        out_specs=pl.BlockSpec((tm, tn), lambda i, j: (i, j)),
        out_shape=jax.ShapeDtypeStruct((T, N), BF16),
        compiler_params=pltpu.CompilerParams(dimension_semantics=("parallel", "parallel"),
                                             vmem_limit_bytes=VMEM_LIMIT_BYTES),
    )(a, w)


_CONV_CHUNK = 256


def _convdown_kernel(u_ref, up_ref, un_ref, cw_ref, cb_ref, wd_ref, x_ref, lnf_ref, o_ref, act_ref, *stage,
                     tm, seq_len, final_norm):
    pos0 = (pl.program_id(0) * tm) % seq_len
    keep_prev = jnp.where(pos0 == 0, 0.0, 1.0)
    keep_next = jnp.where(pos0 + tm == seq_len, 0.0, 1.0)
    lo = SUBLANES

    def conv(c0, buf):
        cols = slice(c0, c0 + _CONV_CHUNK)
        buf[lo:lo + tm, :] = u_ref[:, cols].astype(F32)
        buf[lo - 1:lo, :] = up_ref[SUBLANES - 1:SUBLANES, cols].astype(F32) * keep_prev
        buf[lo + tm:lo + tm + 1, :] = un_ref[0:1, cols].astype(F32) * keep_next
        return (buf[lo - 1:lo - 1 + tm, :] * cw_ref[0:1, cols] + buf[lo:lo + tm, :] * cw_ref[1:2, cols]
                + buf[lo + 1:lo + 1 + tm, :] * cw_ref[2:3, cols] + cb_ref[:, cols])

    for n, c0 in enumerate(range(0, D_FF, _CONV_CHUNK)):
        gate = conv(c0, stage[2 * (n % 2)])
        val = conv(D_FF + c0, stage[2 * (n % 2) + 1])
        half = 0.5 * gate
        act_ref[:, c0:c0 + _CONV_CHUNK] = ((half + half * jnp.tanh(half)) * val).astype(BF16)
    x = x_ref[...] + _dot(act_ref[...], wd_ref[...])
    if final_norm:
        x = _rms(x, lnf_ref[...], EPS)
    o_ref[...] = x


def _conv_down(u, conv_w, conv_b, w_down, x, ln_f, *, seq_len, final_norm):
    T = x.shape[0]
    tm = min(512, T)
    nh = tm // SUBLANES
    n8 = T // SUBLANES
    row = lambda i: (i, 0)
    const = lambda i: (0, 0)
    return pl.pallas_call(
        functools.partial(_convdown_kernel, tm=tm, seq_len=seq_len, final_norm=final_norm),
        grid=(T // tm,),
        in_specs=[pl.BlockSpec((tm, 2 * D_FF), row),
                  pl.BlockSpec((SUBLANES, 2 * D_FF), lambda i: (jnp.maximum(i * nh - 1, 0), 0)),
                  pl.BlockSpec((SUBLANES, 2 * D_FF), lambda i: (jnp.minimum((i + 1) * nh, n8 - 1), 0)),
                  pl.BlockSpec((3, 2 * D_FF), const),
                  pl.BlockSpec((1, 2 * D_FF), const),
                  pl.BlockSpec((D_FF, D_MODEL), const),
                  pl.BlockSpec((tm, D_MODEL), row),
                  pl.BlockSpec((1, D_MODEL), const)],
        out_specs=pl.BlockSpec((tm, D_MODEL), row),
        out_shape=jax.ShapeDtypeStruct((T, D_MODEL), F32),
        scratch_shapes=[pltpu.VMEM((tm, D_FF), BF16)]
                       + [pltpu.VMEM((tm + 2 * SUBLANES, _CONV_CHUNK), F32)] * 4,
        compiler_params=pltpu.CompilerParams(dimension_semantics=("parallel",),
                                             vmem_limit_bytes=VMEM_LIMIT_BYTES),
    )(u, u, u, conv_w, conv_b.reshape(1, 2 * D_FF), w_down, x, ln_f.reshape(1, D_MODEL))


def _trunk(x, p):
    B, S = x.shape[0], x.shape[1]
    xf = x.reshape(B * S, D_MODEL)
    a_slopes = _alibi_slopes(A_HEADS)
    c_slopes = _alibi_slopes(C_GROUPS * C_HEADS)
    for l in range(DEPTH):
        aq, ak, av, bq, bk, bv, *c = _in_proj(xf, p["ln1"][l], p["w_in"][l], B, S)
        as_seq = lambda t: t.reshape(B, 1, S, HEADS_W)
        (oa,) = _banded_attention(as_seq(aq), as_seq(ak), as_seq(av), window=A_WINDOW, slopes_eff=a_slopes,
                                  sink=p["a_sink"][l])
        lam_init = 0.8 - 0.6 * math.exp(-0.3 * l)
        ob = _diff_attention(bq, bk, bv, p["lam"][l], p["subln"][l], B=B, S=S, lam_init=lam_init)
        og, lse = [], []
        for g, (w, d) in enumerate(C_PAIRS):
            o_g, lse_g = _banded_attention(
                c[g], c[3 + g], c[6 + g], window=w // (2 * d),
                slopes_eff=[s * d for s in c_slopes[g * C_HEADS:(g + 1) * C_HEADS]], emit_lse=True)
            og.append(o_g)
            lse.append(lse_g)
        xf, h2 = _mix_out(xf, oa.reshape(B * S, HEADS_W), ob, og, lse, p["w_out"][l], p["ln2"][l], B, S)
        u = _matmul(h2, p["w_up"][l], D_FF)
        xf = _conv_down(u, p["conv_w"][l], p["conv_b"][l], p["w_down"][l], xf, p["ln_f"],
                        seq_len=S, final_norm=(l == DEPTH - 1))
    return xf.reshape(B, S, D_MODEL)


def kernel(x_prompt, x_sample, ln1, w_in, a_sink, lam_q1, lam_k1, lam_q2, lam_k2, subln, w_out, ln2, w_up,
           conv_w, conv_b, w_down, ln_f):
    p = {
        "ln1": ln1,
        "w_in": [_prep_w_in(w_in[l]) for l in range(DEPTH)],
        "a_sink": a_sink,
        "lam": jnp.stack([lam_q1, lam_k1, lam_q2, lam_k2], axis=1),
        "subln": subln,
        "w_out": w_out.astype(BF16),
        "ln2": ln2,
        "w_up": w_up.astype(BF16),
        "conv_w": conv_w,
        "conv_b": conv_b,
        "w_down": w_down.astype(BF16),
        "ln_f": ln_f,
    }
    return (_trunk(x_prompt, p), _trunk(x_sample, p))
```

```python
import functools
import itertools
import math

import numpy as np
import jax
import jax.numpy as jnp
from jax import lax
from jax.experimental import pallas as pl
from jax.experimental.pallas import tpu as pltpu

F32 = jnp.float32
BF16 = jnp.bfloat16

D_MODEL = 1024
DEPTH = 2
HEAD_DIM = 64
BLOCK = 128
EPS = 1e-6
SUBLN_EPS = 1e-5
NEG = -1e30
SCALE = HEAD_DIM ** -0.5
A_HEADS = 4
A_KV_HEADS = 2
A_WINDOW = 128
B_HEADS = 4
B_VDIM = 2 * HEAD_DIM
C_PAIRS = ((128, 1), (512, 4), (2048, 16))
C_GROUPS = 3
C_HEADS = 4
D_FF = 2816
A_Q = A_HEADS * HEAD_DIM
A_KV = A_KV_HEADS * HEAD_DIM
B_QK = B_HEADS * 2 * HEAD_DIM
B_V = B_HEADS * B_VDIM
C_QKV = C_GROUPS * C_HEADS * HEAD_DIM
HEADS_W = 4 * HEAD_DIM

VMEM_LIMIT_BYTES = 56 * 1024 * 1024
LANES = 128
SUBLANES = 8


def _alibi_slopes(n):
    return [2.0 ** (-8.0 * k / n) for k in range(1, n + 1)]


LOG2E = math.log2(math.e)
_AUG_SPLITS = 3


def _bf16_round(x):
    b = np.asarray(x, np.float32).view(np.uint32)
    return ((b + (((b >> 16) & 1) + 0x7FFF)) & 0xFFFF0000).astype(np.uint32).view(np.float32)


def _diff_alibi_tables():
    c = np.asarray(_alibi_slopes(B_HEADS), np.float64) * LOG2E
    pieces, rem = [], c.copy()
    for _ in range(_AUG_SPLITS):
        pk = _bf16_round(rem).astype(np.float64)
        pieces.append(pk)
        rem = rem - pk
    pieces = np.stack(pieces, axis=1)
    ktab = np.zeros((B_HEADS, LANES), np.float32)
    qtab = np.zeros((B_HEADS, 1, LANES), np.float32)
    ktab[:, 0:3] = pieces
    ktab[:, 3:6] = pieces
    qtab[:, 0, 6:9] = pieces
    qtab[:, 0, 9:12] = pieces
    cfull = np.broadcast_to(c.astype(np.float32)[:, None, None], (B_HEADS, 1, LANES)).copy()
    return jnp.asarray(ktab), jnp.asarray(qtab), jnp.asarray(cfull)


def _dot(a, b):
    return jnp.dot(a, b, preferred_element_type=F32)


def _dot_nt(a, b):
    return lax.dot_general(a, b, (((1,), (1,)), ((), ())), preferred_element_type=F32)


def _rms(x, g, eps):
    return x * lax.rsqrt(jnp.mean(x * x, axis=-1, keepdims=True) + eps) * g


_W_AQ, _W_AK, _W_AV = 0, 256, 512
_W_BQ, _W_BK, _W_BV = 768, 1280, 1792
_W_CQ, _W_CK, _W_CV = 2304, 3072, 3840
_W_COLS = 4608


def _inproj_kernel(x_ref, g_ref, w_ref, ktab_ref, aq_ref, ak_ref, av_ref, bq_ref, bk_ref, bv_ref, *rest, seq_len, tm):
    c_refs, slabs = rest[:9], rest[9:]
    h = _rms(x_ref[...], g_ref[...], EPS).astype(BF16)

    def proj(c0, c1):
        return _dot(h, w_ref[:, c0:c1])

    aq_ref[...] = (proj(_W_AQ, _W_AK) * SCALE).astype(BF16)
    ak_ref[...] = proj(_W_AK, _W_AV).astype(BF16)
    av_ref[...] = proj(_W_AV, _W_BQ).astype(BF16)
    bq_ref[...] = (proj(_W_BQ, _W_BK) * (SCALE * LOG2E)).astype(BF16)
    pos = (pl.program_id(0) * tm + lax.broadcasted_iota(jnp.int32, (tm, LANES), 0)) % seq_len
    lane = lax.broadcasted_iota(jnp.int32, (tm, LANES), 1)
    pos_cols = jnp.where(lane < 9, ((pos >> 7) * 128).astype(F32),
                         jnp.where(lane < 12, (pos & 127).astype(F32), 0.0))
    ones = jnp.ones((tm, LANES), BF16)
    bk = proj(_W_BK, _W_BV).astype(BF16)
    bv = proj(_W_BV, _W_CQ).astype(BF16)
    for hh in range(B_HEADS):
        bk_ref[hh, :, 0:LANES] = bk[:, hh * LANES:(hh + 1) * LANES]
        bk_ref[hh, :, LANES:2 * LANES] = jnp.where(lane < 6, ktab_ref[hh:hh + 1, :], pos_cols).astype(BF16)
        bv_ref[hh, :, 0:LANES] = bv[:, hh * LANES:(hh + 1) * LANES]
        bv_ref[hh, :, LANES:2 * LANES] = ones
    slab = 0
    for part, (c0, scale) in enumerate(((_W_CQ, SCALE), (_W_CK, 1.0), (_W_CV, 1.0))):
        pc = proj(c0, c0 + C_QKV) * scale
        for g, (_, d) in enumerate(C_PAIRS):
            dst = c_refs[3 * part + g]
            val = pc[:, g * HEADS_W:(g + 1) * HEADS_W]
            if d == 1:
                dst[0] = val.astype(BF16)
            else:
                for c in range(HEADS_W // LANES):
                    buf = slabs[slab]
                    slab += 1
                    lanes = slice(c * LANES, (c + 1) * LANES)
                    buf[...] = val[:, lanes]
                    for r in range(d):
                        dst[r, :, lanes] = buf[pl.ds(r, tm // d, stride=d), :].astype(BF16)


def _in_proj(x, g, w, B, S):
    T = x.shape[0]
    tm = min(512, S)
    nt = S // tm
    row = lambda i: (i, 0)
    const = lambda i: (0, 0)
    outs = [
        (jax.ShapeDtypeStruct((T, 256), BF16), pl.BlockSpec((tm, 256), row)),
        (jax.ShapeDtypeStruct((T, 256), BF16), pl.BlockSpec((tm, 256), row)),
        (jax.ShapeDtypeStruct((T, 256), BF16), pl.BlockSpec((tm, 256), row)),
        (jax.ShapeDtypeStruct((T, B_QK), BF16), pl.BlockSpec((tm, B_QK), row)),
        (jax.ShapeDtypeStruct((B_HEADS, T, 256), BF16), pl.BlockSpec((B_HEADS, tm, 256), lambda i: (0, i, 0))),
        (jax.ShapeDtypeStruct((B_HEADS, T, 256), BF16), pl.BlockSpec((B_HEADS, tm, 256), lambda i: (0, i, 0))),
    ]
    for _ in range(3):
        for _, d in C_PAIRS:
            outs.append((jax.ShapeDtypeStruct((B, d, S // d, HEADS_W), BF16),
                         pl.BlockSpec((None, d, tm // d, HEADS_W), lambda i: (i // nt, 0, i % nt, 0))))
    n_slabs = 3 * sum(1 for _, d in C_PAIRS if d > 1)
    return pl.pallas_call(
        functools.partial(_inproj_kernel, seq_len=S, tm=tm),
        grid=(T // tm,),
        in_specs=[pl.BlockSpec((tm, D_MODEL), row),
                  pl.BlockSpec((1, D_MODEL), const),
                  pl.BlockSpec((D_MODEL, _W_COLS), const),
                  pl.BlockSpec((B_HEADS, LANES), const)],
        out_specs=[o[1] for o in outs],
        out_shape=[o[0] for o in outs],
        scratch_shapes=[pltpu.VMEM((tm, LANES), F32)] * (n_slabs * (HEADS_W // LANES)),
        compiler_params=pltpu.CompilerParams(dimension_semantics=("parallel",),
                                             vmem_limit_bytes=VMEM_LIMIT_BYTES),
    )(x, g.reshape(1, D_MODEL), w, _diff_alibi_tables()[0])


def _prep_w_in(w):
    g = A_HEADS // A_KV_HEADS
    aq = w[:, :A_Q]
    ak = jnp.repeat(w[:, A_Q:A_Q + A_KV].reshape(D_MODEL, A_KV_HEADS, 1, HEAD_DIM), g, axis=2).reshape(D_MODEL, A_Q)
    av = jnp.repeat(w[:, A_Q + A_KV:A_Q + 2 * A_KV].reshape(D_MODEL, A_KV_HEADS, 1, HEAD_DIM), g, axis=2).reshape(D_MODEL, A_Q)
    return jnp.concatenate([aq, ak, av, w[:, A_Q + 2 * A_KV:]], axis=1).astype(BF16)


_BAND_SUBBLOCKS = 16


def _banded_kernel(*refs, rr, tq, koff, wband, halo, has_sink, emit_lse):
    q_ref, kp_ref, kc_ref, kn_ref, vp_ref, vc_ref, vn_ref, bias_ref, hmask_ref = refs[:9]
    rest = refs[9:]
    if has_sink:
        sink_ref, rest = rest[0], rest[1:]
    o_ref = rest[0]
    lse_ref = rest[1] if emit_lse else None

    first = pl.program_id(2) == 0
    last = pl.program_id(2) == pl.num_programs(2) - 1
    col = lax.broadcasted_iota(jnp.int32, (1, wband), 1)
    lane_head = lax.broadcasted_iota(jnp.int32, (BLOCK, HEADS_W), 1) // HEAD_DIM
    if has_sink:
        sink_col = lax.broadcasted_iota(jnp.int32, (1, LANES), 1) == 0

        def in_col0(x, new):
            return jnp.concatenate([jnp.where(sink_col, new, x[:, :LANES]), x[:, LANES:]], axis=1)
    for r, i in itertools.product(range(rr), range(tq // BLOCK)):
        lo = BLOCK * i - koff
        hi = lo + wband

        def band(p_ref, c_ref, n_ref):
            parts = []
            if lo < 0:
                parts.append(p_ref[r, halo + lo:halo, :])
            parts.append(c_ref[r, max(lo, 0):min(hi, tq), :])
            if hi > tq:
                parts.append(n_ref[r, 0:hi - tq, :])
            return parts[0] if len(parts) == 1 else jnp.concatenate(parts, axis=0)

        kb = band(kp_ref, kc_ref, kn_ref)
        vb = band(vp_ref, vc_ref, vn_ref)
        q = q_ref[r, BLOCK * i:BLOCK * (i + 1), :]
        qs = jnp.concatenate([q] * 4, axis=0) * hmask_ref[...]
        s = _dot_nt(qs, kb) + bias_ref[...]
        if lo < 0:
            s = s + jnp.where(first & (col < -lo), NEG, 0.0)
        if hi > tq:
            s = s + jnp.where(last & (col >= wband - (hi - tq)), NEG, 0.0)
        if has_sink:
            s = in_col0(s, sink_ref[...])
        m = jnp.max(s, axis=-1, keepdims=True)
        p = jnp.exp(s - m)
        l = jnp.sum(p, axis=-1, keepdims=True)
        if has_sink:
            p = in_col0(p, 0.0)
        pv = _dot(p.astype(BF16), vb) / l
        o = jnp.zeros((BLOCK, HEADS_W), F32)
        lse_o = jnp.zeros((BLOCK, HEADS_W), F32)
        lse = m + jnp.log(l) if emit_lse else None
        for h in range(4):
            hr = slice(BLOCK * h, BLOCK * (h + 1))
            o = jnp.where(lane_head == h, pv[hr], o)
            if emit_lse:
                lse_o = jnp.where(lane_head == h, lse[hr], lse_o)
        o_ref[r, BLOCK * i:BLOCK * (i + 1), :] = o.astype(BF16)
        if emit_lse:
            lse_ref[r, BLOCK * i:BLOCK * (i + 1), :] = lse_o


def _band_tables(window, koff, wband, slopes_eff):
    il = np.arange(BLOCK)[:, None]
    c = np.arange(wband)[None, :]
    dist = np.abs(c - koff - il).astype(np.float64)
    bias = np.concatenate([np.where(dist <= window, -s * dist, NEG) for s in slopes_eff], axis=0)
    hmask = np.concatenate([np.broadcast_to((np.arange(HEADS_W) // HEAD_DIM == h)[None, :], (BLOCK, HEADS_W))
                            for h in range(4)], axis=0)
    return jnp.asarray(bias, dtype=F32), jnp.asarray(hmask, dtype=BF16)


def _banded_attention(q, k, v, *, window, slopes_eff, sink=None, emit_lse=False):
    B, d, Sd, _ = q.shape
    tq = min(_BAND_SUBBLOCKS * BLOCK, Sd)
    rr = min(d, max(1, _BAND_SUBBLOCKS * BLOCK // tq))
    koff, wband, halo = (BLOCK // 2, 2 * BLOCK, BLOCK) if window <= BLOCK // 2 else (3 * BLOCK // 2, 4 * BLOCK, 2 * BLOCK)
    halo = min(halo, tq)
    assert koff <= halo and wband - koff - BLOCK <= halo and window <= koff and window <= wband - koff - BLOCK
    nh, nhb = tq // halo, Sd // halo
    bias, hmask = _band_tables(window, koff, wband, slopes_eff)
    cur = pl.BlockSpec((None, rr, tq, HEADS_W), lambda b, r, i: (b, r, i, 0))
    prev = pl.BlockSpec((None, rr, halo, HEADS_W), lambda b, r, i: (b, r, jnp.maximum(i * nh - 1, 0), 0))
    nxt = pl.BlockSpec((None, rr, halo, HEADS_W), lambda b, r, i: (b, r, jnp.minimum((i + 1) * nh, nhb - 1), 0))
    const = lambda b, r, i: (0, 0)
    in_specs = [cur, prev, cur, nxt, prev, cur, nxt,
                pl.BlockSpec((4 * BLOCK, wband), const), pl.BlockSpec((4 * BLOCK, HEADS_W), const)]
    args = [q, k, k, k, v, v, v, bias, hmask]
    if sink is not None:
        assert koff > window
        in_specs.append(pl.BlockSpec((4 * BLOCK, LANES), const))
        args.append(jnp.broadcast_to(jnp.repeat(sink.astype(F32), BLOCK)[:, None], (4 * BLOCK, LANES)))
    out_shape = [jax.ShapeDtypeStruct((B, d, Sd, HEADS_W), BF16)]
    out_specs = [cur]
    if emit_lse:
        out_shape.append(jax.ShapeDtypeStruct((B, d, Sd, HEADS_W), F32))
        out_specs.append(cur)
    return pl.pallas_call(
        functools.partial(_banded_kernel, rr=rr, tq=tq, koff=koff, wband=wband, halo=halo, has_sink=sink is not None,
                          emit_lse=emit_lse),
        grid=(B, d // rr, Sd // tq),
        in_specs=in_specs,
        out_specs=out_specs,
        out_shape=out_shape,
        compiler_params=pltpu.CompilerParams(dimension_semantics=("parallel", "parallel", "parallel"),
                                             vmem_limit_bytes=VMEM_LIMIT_BYTES),
    )(*args)


_SOFTMAX_ROWS = 128
_PIPE_SLOTS = 4


def _diff_kernel(coef_ref, qtab_ref, lam_ref, subln_ref, q_ref, k_ref, v_ref, o_ref,
                 qs_ref, m_ref, acc_ref, *bufs, tq, nk, lam_init):
    qi = pl.program_id(2)
    s_refs, p_refs, a_refs = bufs[0:4], bufs[4:8], bufs[8:12]

    q = q_ref[...].astype(F32)
    lane = lax.broadcasted_iota(jnp.int32, (tq, LANES), 1)
    pos = qi * tq + lax.broadcasted_iota(jnp.int32, (tq, LANES), 0)
    base = jnp.where(lane < 3, -((pos >> 7) * 128).astype(F32),
                     jnp.where(lane < 6, -(pos & 127).astype(F32), qtab_ref[...]))
    q1 = jnp.where(lane < HEAD_DIM, q, 0.0).astype(BF16)
    q2 = jnp.where(lane >= HEAD_DIM, q, 0.0).astype(BF16)
    for var, sign in enumerate((1.0, -1.0, 0.0)):
        aug = (sign * base).astype(BF16)
        qs_ref[var, 0:tq, 0:LANES] = q1
        qs_ref[var, 0:tq, LANES:2 * LANES] = aug
        qs_ref[var, tq:2 * tq, 0:LANES] = q2
        qs_ref[var, tq:2 * tq, LANES:2 * LANES] = aug
    m_ref[...] = jnp.full(m_ref.shape, NEG, F32)
    acc_ref[...] = jnp.zeros(acc_ref.shape, F32)

    def key_tile(t):
        return jnp.where(t == 0, qi, jnp.where(t <= qi, t - 1, t))

    def rows_of(j):
        return pl.ds(pl.multiple_of(j * tq, tq), tq)

    def stage1_diag(slot):
        s = _dot_nt(qs_ref[2], k_ref[rows_of(qi), :])
        r = lax.broadcasted_iota(jnp.int32, s.shape, 0) & (tq - 1)
        c = lax.broadcasted_iota(jnp.int32, s.shape, 1)
        s_refs[slot][...] = s - coef_ref[:, 0:1] * jnp.abs(r - c).astype(F32)

    def stage1(t, slot):
        j = key_tile(t)
        s_refs[slot][...] = _dot_nt(qs_ref[jnp.where(j < qi, 0, 1)], k_ref[rows_of(j), :])

    def stage2(slot):
        s_a, s_b = s_refs[slot], s_refs[slot + 1]
        for r0 in range(0, 2 * tq, _SOFTMAX_ROWS):
            rows = slice(r0, r0 + _SOFTMAX_ROWS)
            m_prev = m_ref[rows, :]
            row_max = jnp.max(jnp.maximum(s_a[rows, :], s_b[rows, :]), axis=-1, keepdims=True)
            m_new = jnp.maximum(m_prev, row_max)
            a_refs[slot][rows, :] = jnp.exp2(m_prev - m_new)
            m_ref[rows, :] = m_new
        for r0 in range(0, 2 * tq, _SOFTMAX_ROWS):
            rows = slice(r0, r0 + _SOFTMAX_ROWS)
            m_new = jnp.concatenate([m_ref[rows, :]] * (tq // LANES), axis=1)
            p_refs[slot][rows, :] = jnp.exp2(s_a[rows, :] - m_new).astype(BF16)
            p_refs[slot + 1][rows, :] = jnp.exp2(s_b[rows, :] - m_new).astype(BF16)

    def stage3(j_a, j_b, slot):
        pv = _dot(p_refs[slot][...], v_ref[rows_of(j_a), :]) + _dot(p_refs[slot + 1][...], v_ref[rows_of(j_b), :])
        a = a_refs[slot][...]
        acc_ref[...] = acc_ref[...] * jnp.concatenate([a, a], axis=1) + pv

    def step(u, first, has1=True, has2=True, has3=True):
        other = 2 - first
        if has2:
            stage2(other)
        if has1:
            stage1(2 * u, first)
            stage1(2 * u + 1, first + 1)
        if has3:
            stage3(key_tile(2 * u - 4), key_tile(2 * u - 3), first)

    stage1_diag(0)
    stage1(1, 1)
    step(1, 2, has3=False)

    for u in range(2, nk // 2):
        step(u, 2 * (u % 2))
    step(nk // 2, 0, has1=False)
    step(nk // 2 + 1, 2, has1=False, has2=False)

    acc = acc_ref[...]
    o1 = acc[:tq, :LANES] / acc[:tq, LANES:]
    o2 = acc[tq:, :LANES] / acc[tq:, LANES:]
    lv = lam_ref[...]
    lam = (jnp.exp(jnp.sum(lv[0:1] * lv[1:2], axis=-1, keepdims=True))
           - jnp.exp(jnp.sum(lv[2:3] * lv[3:4], axis=-1, keepdims=True)) + lam_init)
    o = _rms(o1 - lam * o2, subln_ref[...], SUBLN_EPS) * (1.0 - lam_init)
    o_ref[...] = o.astype(BF16)


def _diff_attention(q, k, v, lam_vecs, subln, *, B, S, lam_init):
    tq = min(512, S // 4)
    nq = S // tq
    assert nq % 4 == 0
    _, qtab, coef = _diff_alibi_tables()
    qv = q.reshape(B, S, B_QK)
    kv = k.reshape(B_HEADS, B, S, 2 * LANES)
    vv = v.reshape(B_HEADS, B, S, 2 * LANES)
    out = pl.pallas_call(
        functools.partial(_diff_kernel, tq=tq, nk=nq, lam_init=lam_init),
        grid=(B, B_HEADS, nq),
        in_specs=[pl.BlockSpec((None, 1, LANES), lambda b, h, i: (h, 0, 0)),
                  pl.BlockSpec((None, 1, LANES), lambda b, h, i: (h, 0, 0)),
                  pl.BlockSpec((4, HEAD_DIM), lambda b, h, i: (0, 0)),
                  pl.BlockSpec((1, B_VDIM), lambda b, h, i: (0, 0)),
                  pl.BlockSpec((None, tq, LANES), lambda b, h, i: (b, i, h)),
                  pl.BlockSpec((None, None, S, 2 * LANES), lambda b, h, i: (h, b, 0, 0)),
                  pl.BlockSpec((None, None, S, 2 * LANES), lambda b, h, i: (h, b, 0, 0))],
        out_specs=pl.BlockSpec((None, tq, LANES), lambda b, h, i: (b, i, h)),
        out_shape=jax.ShapeDtypeStruct((B, S, B_V), BF16),
        scratch_shapes=[pltpu.VMEM((3, 2 * tq, 2 * LANES), BF16),
                        pltpu.VMEM((2 * tq, LANES), F32),
                        pltpu.VMEM((2 * tq, 2 * LANES), F32),
                        *[pltpu.VMEM((2 * tq, tq), F32)] * _PIPE_SLOTS,
                        *[pltpu.VMEM((2 * tq, tq), BF16)] * _PIPE_SLOTS,
                        *[pltpu.VMEM((2 * tq, LANES), F32)] * _PIPE_SLOTS],
        compiler_params=pltpu.CompilerParams(
            dimension_semantics=("parallel", "parallel", "arbitrary"),
            vmem_limit_bytes=VMEM_LIMIT_BYTES),
    )(coef, qtab, lam_vecs.astype(F32), subln.reshape(1, B_VDIM).astype(F32), qv, kv, vv)
    return out.reshape(B * S, B_V)


def _mixout_kernel(x_ref, oa_ref, ob_ref, g0_ref, g1_ref, g2_ref, l0_ref, l1_ref, l2_ref, w_ref, ln_ref,
                   xo_ref, h_ref, *slabs, tm):
    slabs = list(slabs)

    def tokens(ref, d):
        if d == 1:
            return ref[0].astype(F32)
        halves = []
        for c in range(HEADS_W // LANES):
            buf = slabs.pop()
            for r in range(d):
                buf[pl.ds(r, tm // d, stride=d), :] = ref[r, :, c * LANES:(c + 1) * LANES].astype(F32)
            halves.append(buf[...])
        return jnp.concatenate(halves, axis=1)

    dil = [d for _, d in C_PAIRS]
    l0, l1, l2 = tokens(l0_ref, dil[0]), tokens(l1_ref, dil[1]), tokens(l2_ref, dil[2])
    g0, g1, g2 = tokens(g0_ref, dil[0]), tokens(g1_ref, dil[1]), tokens(g2_ref, dil[2])
    mx = jnp.maximum(jnp.maximum(l0, l1), l2)
    e0, e1, e2 = jnp.exp(l0 - mx), jnp.exp(l1 - mx), jnp.exp(l2 - mx)
    oc = (e0 * g0 + e1 * g1 + e2 * g2) / (e0 + e1 + e2)
    mix = (_dot(oa_ref[...], w_ref[0:A_Q, :])
           + _dot(ob_ref[...], w_ref[A_Q:A_Q + B_V, :])
           + _dot(oc.astype(BF16), w_ref[A_Q + B_V:, :]))
    x = x_ref[...] + mix
    xo_ref[...] = x
    h_ref[...] = _rms(x, ln_ref[...], EPS).astype(BF16)


def _mix_out(x, oa, ob, og, lse, w_out, ln2, B, S):
    T = x.shape[0]
    tm = min(512, S)
    nt = S // tm
    row = lambda i: (i, 0)
    const = lambda i: (0, 0)
    grp = [pl.BlockSpec((None, d, tm // d, HEADS_W), lambda i: (i // nt, 0, i % nt, 0)) for _, d in C_PAIRS]
    n_slabs = 2 * sum(1 for _, d in C_PAIRS if d > 1)
    return pl.pallas_call(
        functools.partial(_mixout_kernel, tm=tm),
        grid=(T // tm,),
        in_specs=[pl.BlockSpec((tm, D_MODEL), row),
                  pl.BlockSpec((tm, A_Q), row),
                  pl.BlockSpec((tm, B_V), row)]
                 + grp + grp
                 + [pl.BlockSpec((D_MODEL, D_MODEL), const),
                    pl.BlockSpec((1, D_MODEL), const)],
        out_specs=[pl.BlockSpec((tm, D_MODEL), row), pl.BlockSpec((tm, D_MODEL), row)],
        out_shape=[jax.ShapeDtypeStruct((T, D_MODEL), F32), jax.ShapeDtypeStruct((T, D_MODEL), BF16)],
        scratch_shapes=[pltpu.VMEM((tm, LANES), F32)] * (n_slabs * (HEADS_W // LANES)),
        compiler_params=pltpu.CompilerParams(dimension_semantics=("parallel",),
                                             vmem_limit_bytes=VMEM_LIMIT_BYTES),
    )(x, oa, ob, *og, *lse, w_out, ln2.reshape(1, D_MODEL))


_UP_CHUNK = 256
_UP_HALO = 16


def _upgate_kernel(hp_ref, hc_ref, hn_ref, w_ref, cw_ref, cb_ref, o_ref, hh_ref, *, tm, seq_len):
    pos0 = (pl.program_id(0) * tm) % seq_len
    keep_prev = jnp.where(pos0 == 0, 0.0, 1.0)
    keep_next = jnp.where(pos0 + tm == seq_len, 0.0, 1.0)
    ext = tm + 2 * _UP_HALO
    hh_ref[0:_UP_HALO, :] = (hp_ref[...].astype(F32) * keep_prev).astype(BF16)
    hh_ref[_UP_HALO:_UP_HALO + tm, :] = hc_ref[...]
    hh_ref[_UP_HALO + tm:ext, :] = (hn_ref[...].astype(F32) * keep_next).astype(BF16)
    hh = hh_ref[...]

    def conv(c0):
        cols = slice(c0, c0 + _UP_CHUNK)
        u = _dot(hh, w_ref[:, cols])
        return (pltpu.roll(u, 1, axis=0) * cw_ref[0:1, cols] + u * cw_ref[1:2, cols]
                + pltpu.roll(u, ext - 1, axis=0) * cw_ref[2:3, cols] + cb_ref[:, cols])

    for c0 in range(0, D_FF, _UP_CHUNK):
        gate = conv(c0)
        val = conv(D_FF + c0)
        half = 0.5 * gate
        act = (half + half * jnp.tanh(half)) * val
        o_ref[:, c0:c0 + _UP_CHUNK] = act[_UP_HALO:_UP_HALO + tm, :].astype(BF16)


def _up_gate(h, w_up, conv_w, conv_b, *, seq_len):
    T = h.shape[0]
    tm = min(512, T)
    nh = tm // _UP_HALO
    n_halo = T // _UP_HALO
    row = lambda i: (i, 0)
    const = lambda i: (0, 0)
    return pl.pallas_call(
        functools.partial(_upgate_kernel, tm=tm, seq_len=seq_len),
        grid=(T // tm,),
        in_specs=[pl.BlockSpec((_UP_HALO, D_MODEL), lambda i: (jnp.maximum(i * nh - 1, 0), 0)),
                  pl.BlockSpec((tm, D_MODEL), row),
                  pl.BlockSpec((_UP_HALO, D_MODEL), lambda i: (jnp.minimum((i + 1) * nh, n_halo - 1), 0)),
                  pl.BlockSpec((D_MODEL, 2 * D_FF), const),
                  pl.BlockSpec((3, 2 * D_FF), const),
                  pl.BlockSpec((1, 2 * D_FF), const)],
        out_specs=pl.BlockSpec((tm, D_FF), row),
        out_shape=jax.ShapeDtypeStruct((T, D_FF), BF16),
        scratch_shapes=[pltpu.VMEM((tm + 2 * _UP_HALO, D_MODEL), BF16)],
        compiler_params=pltpu.CompilerParams(dimension_semantics=("parallel",),
                                             vmem_limit_bytes=VMEM_LIMIT_BYTES),
    )(h, h, h, w_up, conv_w, conv_b.reshape(1, 2 * D_FF))


def _down_kernel(a_ref, w_ref, x_ref, lnf_ref, o_ref, *, final_norm):
    x = x_ref[...] + _dot(a_ref[...], w_ref[...])
    if final_norm:
        x = _rms(x, lnf_ref[...], EPS)
    o_ref[...] = x


def _down(act, w_down, x, ln_f, *, final_norm):
    T = x.shape[0]
    tm = min(512, T)
    row = lambda i: (i, 0)
    const = lambda i: (0, 0)
    return pl.pallas_call(
        functools.partial(_down_kernel, final_norm=final_norm),
        grid=(T // tm,),
        in_specs=[pl.BlockSpec((tm, D_FF), row),
                  pl.BlockSpec((D_FF, D_MODEL), const),
                  pl.BlockSpec((tm, D_MODEL), row),
                  pl.BlockSpec((1, D_MODEL), const)],
        out_specs=pl.BlockSpec((tm, D_MODEL), row),
        out_shape=jax.ShapeDtypeStruct((T, D_MODEL), F32),
        compiler_params=pltpu.CompilerParams(dimension_semantics=("parallel",),
                                             vmem_limit_bytes=VMEM_LIMIT_BYTES),
    )(act, w_down, x, ln_f.reshape(1, D_MODEL))


def _trunk(x, p):
    B, S = x.shape[0], x.shape[1]
    xf = x.reshape(B * S, D_MODEL)
    a_slopes = _alibi_slopes(A_HEADS)
    c_slopes = _alibi_slopes(C_GROUPS * C_HEADS)
    for l in range(DEPTH):
        aq, ak, av, bq, bk, bv, *c = _in_proj(xf, p["ln1"][l], p["w_in"][l], B, S)
        as_seq = lambda t: t.reshape(B, 1, S, HEADS_W)
        (oa,) = _banded_attention(as_seq(aq), as_seq(ak), as_seq(av), window=A_WINDOW, slopes_eff=a_slopes,
                                  sink=p["a_sink"][l])
        lam_init = 0.8 - 0.6 * math.exp(-0.3 * l)
        ob = _diff_attention(bq, bk, bv, p["lam"][l], p["subln"][l], B=B, S=S, lam_init=lam_init)
        og, lse = [], []
        for g, (w, d) in enumerate(C_PAIRS):
            o_g, lse_g = _banded_attention(
                c[g], c[3 + g], c[6 + g], window=w // (2 * d),
                slopes_eff=[s * d for s in c_slopes[g * C_HEADS:(g + 1) * C_HEADS]], emit_lse=True)
            og.append(o_g)
            lse.append(lse_g)
        xf, h2 = _mix_out(xf, oa.reshape(B * S, HEADS_W), ob, og, lse, p["w_out"][l], p["ln2"][l], B, S)
        act = _up_gate(h2, p["w_up"][l], p["conv_w"][l], p["conv_b"][l], seq_len=S)
        xf = _down(act, p["w_down"][l], xf, p["ln_f"], final_norm=(l == DEPTH - 1))
    return xf.reshape(B, S, D_MODEL)


def kernel(x_prompt, x_sample, ln1, w_in, a_sink, lam_q1, lam_k1, lam_q2, lam_k2, subln, w_out, ln2, w_up,
           conv_w, conv_b, w_down, ln_f):
    p = {
        "ln1": ln1,
        "w_in": [_prep_w_in(w_in[l]) for l in range(DEPTH)],
        "a_sink": a_sink,
        "lam": jnp.stack([lam_q1, lam_k1, lam_q2, lam_k2], axis=1),
        "subln": subln,
        "w_out": w_out.astype(BF16),
        "ln2": ln2,
        "w_up": w_up.astype(BF16),
        "conv_w": conv_w,
        "conv_b": conv_b,
        "w_down": w_down.astype(BF16),
        "ln_f": ln_f,
    }
    return (_trunk(x_prompt, p), _trunk(x_sample, p))
```

```python
import functools
import itertools
import math

import numpy as np
import jax
import jax.numpy as jnp
from jax import lax
from jax.experimental import pallas as pl
from jax.experimental.pallas import tpu as pltpu

F32 = jnp.float32
BF16 = jnp.bfloat16

D_MODEL = 1024
DEPTH = 2
HEAD_DIM = 64
BLOCK = 128
EPS = 1e-6
SUBLN_EPS = 1e-5
NEG = -1e30
SCALE = HEAD_DIM ** -0.5
A_HEADS = 4
A_KV_HEADS = 2
A_WINDOW = 128
B_HEADS = 4
B_VDIM = 2 * HEAD_DIM
C_PAIRS = ((128, 1), (512, 4), (2048, 16))
C_GROUPS = 3
C_HEADS = 4
D_FF = 2816
A_Q = A_HEADS * HEAD_DIM
A_KV = A_KV_HEADS * HEAD_DIM
B_QK = B_HEADS * 2 * HEAD_DIM
B_V = B_HEADS * B_VDIM
C_QKV = C_GROUPS * C_HEADS * HEAD_DIM
HEADS_W = 4 * HEAD_DIM

VMEM_LIMIT_BYTES = 56 * 1024 * 1024
LANES = 128
SUBLANES = 8


def _alibi_slopes(n):
    return [2.0 ** (-8.0 * k / n) for k in range(1, n + 1)]


LOG2E = math.log2(math.e)
_AUG_SPLITS = 3


def _bf16_round(x):
    b = np.asarray(x, np.float32).view(np.uint32)
    return ((b + (((b >> 16) & 1) + 0x7FFF)) & 0xFFFF0000).astype(np.uint32).view(np.float32)


def _diff_alibi_tables():
    c = np.asarray(_alibi_slopes(B_HEADS), np.float64) * LOG2E
    pieces, rem = [], c.copy()
    for _ in range(_AUG_SPLITS):
        pk = _bf16_round(rem).astype(np.float64)
        pieces.append(pk)
        rem = rem - pk
    pieces = np.stack(pieces, axis=1)
    ktab = np.zeros((B_HEADS, LANES), np.float32)
    qtab = np.zeros((B_HEADS, 1, LANES), np.float32)
    ktab[:, 0:3] = pieces
    ktab[:, 3:6] = pieces
    qtab[:, 0, 6:9] = pieces
    qtab[:, 0, 9:12] = pieces
    cfull = np.broadcast_to(c.astype(np.float32)[:, None, None], (B_HEADS, 1, LANES)).copy()
    return jnp.asarray(ktab), jnp.asarray(qtab), jnp.asarray(cfull)


def _dot(a, b):
    return jnp.dot(a, b, preferred_element_type=F32)


def _dot_nt(a, b):
    return lax.dot_general(a, b, (((1,), (1,)), ((), ())), preferred_element_type=F32)


def _rms(x, g, eps):
    return x * lax.rsqrt(jnp.mean(x * x, axis=-1, keepdims=True) + eps) * g


_W_AQ, _W_AK, _W_AV = 0, 256, 512
_W_BQ, _W_BK, _W_BV = 768, 1280, 1792
_W_CQ, _W_CK, _W_CV = 2304, 3072, 3840
_W_COLS = 4608


def _inproj_kernel(x_ref, g_ref, w_ref, ktab_ref, aq_ref, ak_ref, av_ref, bq_ref, bk_ref, bv_ref, *rest, seq_len, tm):
    c_refs, slabs = rest[:9], rest[9:]
    h = _rms(x_ref[...], g_ref[...], EPS).astype(BF16)

    def proj(c0, c1):
        return _dot(h, w_ref[:, c0:c1])

    aq_ref[...] = (proj(_W_AQ, _W_AK) * SCALE).astype(BF16)
    ak_ref[...] = proj(_W_AK, _W_AV).astype(BF16)
    av_ref[...] = proj(_W_AV, _W_BQ).astype(BF16)
    bq_ref[...] = (proj(_W_BQ, _W_BK) * (SCALE * LOG2E)).astype(BF16)
    pos = (pl.program_id(0) * tm + lax.broadcasted_iota(jnp.int32, (tm, LANES), 0)) % seq_len
    lane = lax.broadcasted_iota(jnp.int32, (tm, LANES), 1)
    pos_cols = jnp.where(lane < 9, ((pos >> 7) * 128).astype(F32),
                         jnp.where(lane < 12, (pos & 127).astype(F32), 0.0))
    ones = jnp.ones((tm, LANES), BF16)
    bk = proj(_W_BK, _W_BV).astype(BF16)
    bv = proj(_W_BV, _W_CQ).astype(BF16)
    for hh in range(B_HEADS):
        bk_ref[hh, :, 0:LANES] = bk[:, hh * LANES:(hh + 1) * LANES]
        bk_ref[hh, :, LANES:2 * LANES] = jnp.where(lane < 6, ktab_ref[hh:hh + 1, :], pos_cols).astype(BF16)
        bv_ref[hh, :, 0:LANES] = bv[:, hh * LANES:(hh + 1) * LANES]
        bv_ref[hh, :, LANES:2 * LANES] = ones
    slab = 0
    for part, (c0, scale) in enumerate(((_W_CQ, SCALE), (_W_CK, 1.0), (_W_CV, 1.0))):
        pc = proj(c0, c0 + C_QKV) * scale
        for g, (_, d) in enumerate(C_PAIRS):
            dst = c_refs[3 * part + g]
            val = pc[:, g * HEADS_W:(g + 1) * HEADS_W]
            if d == 1:
                dst[0] = val.astype(BF16)
            else:
                for c in range(HEADS_W // LANES):
                    buf = slabs[slab]
                    slab += 1
                    lanes = slice(c * LANES, (c + 1) * LANES)
                    buf[...] = val[:, lanes]
                    for r in range(d):
                        dst[r, :, lanes] = buf[pl.ds(r, tm // d, stride=d), :].astype(BF16)


def _in_proj(x, g, w, B, S):
    T = x.shape[0]
    tm = min(512, S)
    nt = S // tm
    row = lambda i: (i, 0)
    const = lambda i: (0, 0)
    outs = [
        (jax.ShapeDtypeStruct((T, 256), BF16), pl.BlockSpec((tm, 256), row)),
        (jax.ShapeDtypeStruct((T, 256), BF16), pl.BlockSpec((tm, 256), row)),
        (jax.ShapeDtypeStruct((T, 256), BF16), pl.BlockSpec((tm, 256), row)),
        (jax.ShapeDtypeStruct((T, B_QK), BF16), pl.BlockSpec((tm, B_QK), row)),
        (jax.ShapeDtypeStruct((B_HEADS, T, 256), BF16), pl.BlockSpec((B_HEADS, tm, 256), lambda i: (0, i, 0))),
        (jax.ShapeDtypeStruct((B_HEADS, T, 256), BF16), pl.BlockSpec((B_HEADS, tm, 256), lambda i: (0, i, 0))),
    ]
    for _ in range(3):
        for _, d in C_PAIRS:
            outs.append((jax.ShapeDtypeStruct((B, d, S // d, HEADS_W), BF16),
                         pl.BlockSpec((None, d, tm // d, HEADS_W), lambda i: (i // nt, 0, i % nt, 0))))
    n_slabs = 3 * sum(1 for _, d in C_PAIRS if d > 1)
    return pl.pallas_call(
        functools.partial(_inproj_kernel, seq_len=S, tm=tm),
        grid=(T // tm,),
        in_specs=[pl.BlockSpec((tm, D_MODEL), row),
                  pl.BlockSpec((1, D_MODEL), const),
                  pl.BlockSpec((D_MODEL, _W_COLS), const),
                  pl.BlockSpec((B_HEADS, LANES), const)],
        out_specs=[o[1] for o in outs],
        out_shape=[o[0] for o in outs],
        scratch_shapes=[pltpu.VMEM((tm, LANES), F32)] * (n_slabs * (HEADS_W // LANES)),
        compiler_params=pltpu.CompilerParams(dimension_semantics=("parallel",),
                                             vmem_limit_bytes=VMEM_LIMIT_BYTES),
    )(x, g.reshape(1, D_MODEL), w, _diff_alibi_tables()[0])


def _prep_w_in(w):
    g = A_HEADS // A_KV_HEADS
    aq = w[:, :A_Q]
    ak = jnp.repeat(w[:, A_Q:A_Q + A_KV].reshape(D_MODEL, A_KV_HEADS, 1, HEAD_DIM), g, axis=2).reshape(D_MODEL, A_Q)
    av = jnp.repeat(w[:, A_Q + A_KV:A_Q + 2 * A_KV].reshape(D_MODEL, A_KV_HEADS, 1, HEAD_DIM), g, axis=2).reshape(D_MODEL, A_Q)
    return jnp.concatenate([aq, ak, av, w[:, A_Q + 2 * A_KV:]], axis=1).astype(BF16)


_BAND_SUBBLOCKS = 16


def _banded_kernel(*refs, rr, tq, koff, wband, halo, has_sink, emit_lse):
    q_ref, kp_ref, kc_ref, kn_ref, vp_ref, vc_ref, vn_ref, bias_ref, hmask_ref = refs[:9]
    rest = refs[9:]
    if has_sink:
        sink_ref, rest = rest[0], rest[1:]
    o_ref = rest[0]
    lse_ref = rest[1] if emit_lse else None

    first = pl.program_id(2) == 0
    last = pl.program_id(2) == pl.num_programs(2) - 1
    col = lax.broadcasted_iota(jnp.int32, (1, wband), 1)
    lane_head = lax.broadcasted_iota(jnp.int32, (BLOCK, HEADS_W), 1) // HEAD_DIM
    if has_sink:
        sink_col = lax.broadcasted_iota(jnp.int32, (1, LANES), 1) == 0

        def in_col0(x, new):
            return jnp.concatenate([jnp.where(sink_col, new, x[:, :LANES]), x[:, LANES:]], axis=1)
    for r, i in itertools.product(range(rr), range(tq // BLOCK)):
        lo = BLOCK * i - koff
        hi = lo + wband

        def band(p_ref, c_ref, n_ref):
            parts = []
            if lo < 0:
                parts.append(p_ref[r, halo + lo:halo, :])
            parts.append(c_ref[r, max(lo, 0):min(hi, tq), :])
            if hi > tq:
                parts.append(n_ref[r, 0:hi - tq, :])
            return parts[0] if len(parts) == 1 else jnp.concatenate(parts, axis=0)

        kb = band(kp_ref, kc_ref, kn_ref)
        vb = band(vp_ref, vc_ref, vn_ref)
        q = q_ref[r, BLOCK * i:BLOCK * (i + 1), :]
        qs = jnp.concatenate([q] * 4, axis=0) * hmask_ref[...]
        s = _dot_nt(qs, kb) + bias_ref[...]
        if lo < 0:
            s = s + jnp.where(first & (col < -lo), NEG, 0.0)
        if hi > tq:
            s = s + jnp.where(last & (col >= wband - (hi - tq)), NEG, 0.0)
        if has_sink:
            s = in_col0(s, sink_ref[...])
        m = jnp.max(s, axis=-1, keepdims=True)
        p = jnp.exp(s - m)
        l = jnp.sum(p, axis=-1, keepdims=True)
        if has_sink:
            p = in_col0(p, 0.0)
        pv = _dot(p.astype(BF16), vb) / l
        o = jnp.zeros((BLOCK, HEADS_W), F32)
        lse_o = jnp.zeros((BLOCK, HEADS_W), F32)
        lse = m + jnp.log(l) if emit_lse else None
        for h in range(4):
            hr = slice(BLOCK * h, BLOCK * (h + 1))
            o = jnp.where(lane_head == h, pv[hr], o)
            if emit_lse:
                lse_o = jnp.where(lane_head == h, lse[hr], lse_o)
        o_ref[r, BLOCK * i:BLOCK * (i + 1), :] = o.astype(BF16)
        if emit_lse:
            lse_ref[r, BLOCK * i:BLOCK * (i + 1), :] = lse_o


def _band_tables(window, koff, wband, slopes_eff):
    il = np.arange(BLOCK)[:, None]
    c = np.arange(wband)[None, :]
    dist = np.abs(c - koff - il).astype(np.float64)
    bias = np.concatenate([np.where(dist <= window, -s * dist, NEG) for s in slopes_eff], axis=0)
    hmask = np.concatenate([np.broadcast_to((np.arange(HEADS_W) // HEAD_DIM == h)[None, :], (BLOCK, HEADS_W))
                            for h in range(4)], axis=0)
    return jnp.asarray(bias, dtype=F32), jnp.asarray(hmask, dtype=BF16)


def _banded_attention(q, k, v, *, window, slopes_eff, sink=None, emit_lse=False):
    B, d, Sd, _ = q.shape
    tq = min(_BAND_SUBBLOCKS * BLOCK, Sd)
    rr = min(d, max(1, _BAND_SUBBLOCKS * BLOCK // tq))
    koff, wband, halo = (BLOCK // 2, 2 * BLOCK, BLOCK) if window <= BLOCK // 2 else (3 * BLOCK // 2, 4 * BLOCK, 2 * BLOCK)
    halo = min(halo, tq)
    assert koff <= halo and wband - koff - BLOCK <= halo and window <= koff and window <= wband - koff - BLOCK
    nh, nhb = tq // halo, Sd // halo
    bias, hmask = _band_tables(window, koff, wband, slopes_eff)
    cur = pl.BlockSpec((None, rr, tq, HEADS_W), lambda b, r, i: (b, r, i, 0))
    prev = pl.BlockSpec((None, rr, halo, HEADS_W), lambda b, r, i: (b, r, jnp.maximum(i * nh - 1, 0), 0))
    nxt = pl.BlockSpec((None, rr, halo, HEADS_W), lambda b, r, i: (b, r, jnp.minimum((i + 1) * nh, nhb - 1), 0))
    const = lambda b, r, i: (0, 0)
    in_specs = [cur, prev, cur, nxt, prev, cur, nxt,
                pl.BlockSpec((4 * BLOCK, wband), const), pl.BlockSpec((4 * BLOCK, HEADS_W), const)]
    args = [q, k, k, k, v, v, v, bias, hmask]
    if sink is not None:
        assert koff > window
        in_specs.append(pl.BlockSpec((4 * BLOCK, LANES), const))
        args.append(jnp.broadcast_to(jnp.repeat(sink.astype(F32), BLOCK)[:, None], (4 * BLOCK, LANES)))
    out_shape = [jax.ShapeDtypeStruct((B, d, Sd, HEADS_W), BF16)]
    out_specs = [cur]
    if emit_lse:
        out_shape.append(jax.ShapeDtypeStruct((B, d, Sd, HEADS_W), F32))
        out_specs.append(cur)
    return pl.pallas_call(
        functools.partial(_banded_kernel, rr=rr, tq=tq, koff=koff, wband=wband, halo=halo, has_sink=sink is not None,
                          emit_lse=emit_lse),
        grid=(B, d // rr, Sd // tq),
        in_specs=in_specs,
        out_specs=out_specs,
        out_shape=out_shape,
        compiler_params=pltpu.CompilerParams(dimension_semantics=("parallel", "parallel", "parallel"),
                                             vmem_limit_bytes=VMEM_LIMIT_BYTES),
    )(*args)


_SOFTMAX_ROWS = 128
_PIPE_SLOTS = 4


def _diff_kernel(coef_ref, qtab_ref, lam_ref, subln_ref, q_ref, k_ref, v_ref, o_ref,
                 qs_ref, m_ref, acc_ref, *bufs, tq, nk, lam_init):
    qi = pl.program_id(2)
    s_refs, p_refs, a_refs = bufs[0:4], bufs[4:8], bufs[8:12]

    q = q_ref[...].astype(F32)
    lane = lax.broadcasted_iota(jnp.int32, (tq, LANES), 1)
    pos = qi * tq + lax.broadcasted_iota(jnp.int32, (tq, LANES), 0)
    base = jnp.where(lane < 3, -((pos >> 7) * 128).astype(F32),
                     jnp.where(lane < 6, -(pos & 127).astype(F32), qtab_ref[...]))
    q1 = jnp.where(lane < HEAD_DIM, q, 0.0).astype(BF16)
    q2 = jnp.where(lane >= HEAD_DIM, q, 0.0).astype(BF16)
    for var, sign in enumerate((1.0, -1.0, 0.0)):
        aug = (sign * base).astype(BF16)
        qs_ref[var, 0:tq, 0:LANES] = q1
        qs_ref[var, 0:tq, LANES:2 * LANES] = aug
        qs_ref[var, tq:2 * tq, 0:LANES] = q2
        qs_ref[var, tq:2 * tq, LANES:2 * LANES] = aug
    m_ref[...] = jnp.full(m_ref.shape, NEG, F32)
    acc_ref[...] = jnp.zeros(acc_ref.shape, F32)

    def key_tile(t):
        return jnp.where(t == 0, qi, jnp.where(t <= qi, t - 1, t))

    def rows_of(j):
        return pl.ds(pl.multiple_of(j * tq, tq), tq)

    def stage1_diag(slot):
        s = _dot_nt(qs_ref[2], k_ref[rows_of(qi), :])
        r = lax.broadcasted_iota(jnp.int32, s.shape, 0) & (tq - 1)
        c = lax.broadcasted_iota(jnp.int32, s.shape, 1)
        s_refs[slot][...] = s - coef_ref[:, 0:1] * jnp.abs(r - c).astype(F32)

    def stage1(t, slot):
        j = key_tile(t)
        s_refs[slot][...] = _dot_nt(qs_ref[jnp.where(j < qi, 0, 1)], k_ref[rows_of(j), :])

    def stage2(slot):
        s_a, s_b = s_refs[slot], s_refs[slot + 1]
        for r0 in range(0, 2 * tq, _SOFTMAX_ROWS):
            rows = slice(r0, r0 + _SOFTMAX_ROWS)
            m_prev = m_ref[rows, :]
            row_max = jnp.max(jnp.maximum(s_a[rows, :], s_b[rows, :]), axis=-1, keepdims=True)
            m_new = jnp.maximum(m_prev, row_max)
            a_refs[slot][rows, :] = jnp.exp2(m_prev - m_new)
            m_ref[rows, :] = m_new
        for r0 in range(0, 2 * tq, _SOFTMAX_ROWS):
            rows = slice(r0, r0 + _SOFTMAX_ROWS)
            m_new = jnp.concatenate([m_ref[rows, :]] * (tq // LANES), axis=1)
            p_refs[slot][rows, :] = jnp.exp2((s_a[rows, :] - m_new).astype(BF16))
            p_refs[slot + 1][rows, :] = jnp.exp2((s_b[rows, :] - m_new).astype(BF16))

    def stage3(j_a, j_b, slot):
        pv = _dot(p_refs[slot][...], v_ref[rows_of(j_a), :]) + _dot(p_refs[slot + 1][...], v_ref[rows_of(j_b), :])
        a = a_refs[slot][...]
        acc_ref[...] = acc_ref[...] * jnp.concatenate([a, a], axis=1) + pv

    def step(u, first, has1=True, has2=True, has3=True):
        other = 2 - first
        if has2:
            stage2(other)
        if has1:
            stage1(2 * u, first)
            stage1(2 * u + 1, first + 1)
        if has3:
            stage3(key_tile(2 * u - 4), key_tile(2 * u - 3), first)

    stage1_diag(0)
    stage1(1, 1)
    step(1, 2, has3=False)

    for u in range(2, nk // 2):
        step(u, 2 * (u % 2))
    step(nk // 2, 0, has1=False)
    step(nk // 2 + 1, 2, has1=False, has2=False)

    acc = acc_ref[...]
    o1 = acc[:tq, :LANES] / acc[:tq, LANES:]
    o2 = acc[tq:, :LANES] / acc[tq:, LANES:]
    lv = lam_ref[...]
    lam = (jnp.exp(jnp.sum(lv[0:1] * lv[1:2], axis=-1, keepdims=True))
           - jnp.exp(jnp.sum(lv[2:3] * lv[3:4], axis=-1, keepdims=True)) + lam_init)
    o = _rms(o1 - lam * o2, subln_ref[...], SUBLN_EPS) * (1.0 - lam_init)
    o_ref[...] = o.astype(BF16)


def _diff_attention(q, k, v, lam_vecs, subln, *, B, S, lam_init):
    tq = min(512, S // 4)
    nq = S // tq
    assert nq % 4 == 0
    _, qtab, coef = _diff_alibi_tables()
    qv = q.reshape(B, S, B_QK)
    kv = k.reshape(B_HEADS, B, S, 2 * LANES)
    vv = v.reshape(B_HEADS, B, S, 2 * LANES)
    out = pl.pallas_call(
        functools.partial(_diff_kernel, tq=tq, nk=nq, lam_init=lam_init),
        grid=(B, B_HEADS, nq),
        in_specs=[pl.BlockSpec((None, 1, LANES), lambda b, h, i: (h, 0, 0)),
                  pl.BlockSpec((None, 1, LANES), lambda b, h, i: (h, 0, 0)),
                  pl.BlockSpec((4, HEAD_DIM), lambda b, h, i: (0, 0)),
                  pl.BlockSpec((1, B_VDIM), lambda b, h, i: (0, 0)),
                  pl.BlockSpec((None, tq, LANES), lambda b, h, i: (b, i, h)),
                  pl.BlockSpec((None, None, S, 2 * LANES), lambda b, h, i: (h, b, 0, 0)),
                  pl.BlockSpec((None, None, S, 2 * LANES), lambda b, h, i: (h, b, 0, 0))],
        out_specs=pl.BlockSpec((None, tq, LANES), lambda b, h, i: (b, i, h)),
        out_shape=jax.ShapeDtypeStruct((B, S, B_V), BF16),
        scratch_shapes=[pltpu.VMEM((3, 2 * tq, 2 * LANES), BF16),
                        pltpu.VMEM((2 * tq, LANES), F32),
                        pltpu.VMEM((2 * tq, 2 * LANES), F32),
                        *[pltpu.VMEM((2 * tq, tq), F32)] * _PIPE_SLOTS,
                        *[pltpu.VMEM((2 * tq, tq), BF16)] * _PIPE_SLOTS,
                        *[pltpu.VMEM((2 * tq, LANES), F32)] * _PIPE_SLOTS],
        compiler_params=pltpu.CompilerParams(
            dimension_semantics=("parallel", "parallel", "arbitrary"),
            vmem_limit_bytes=VMEM_LIMIT_BYTES),
    )(coef, qtab, lam_vecs.astype(F32), subln.reshape(1, B_VDIM).astype(F32), qv, kv, vv)
    return out.reshape(B * S, B_V)


def _mixout_kernel(x_ref, oa_ref, ob_ref, g0_ref, g1_ref, g2_ref, l0_ref, l1_ref, l2_ref, w_ref, ln_ref,
                   xo_ref, h_ref, *slabs, tm):
    slabs = list(slabs)

    def tokens(ref, d):
        if d == 1:
            return ref[0].astype(F32)
        halves = []
        for c in range(HEADS_W // LANES):
            buf = slabs.pop()
            for r in range(d):
                buf[pl.ds(r, tm // d, stride=d), :] = ref[r, :, c * LANES:(c + 1) * LANES].astype(F32)
            halves.append(buf[...])
        return jnp.concatenate(halves, axis=1)

    dil = [d for _, d in C_PAIRS]
    l0, l1, l2 = tokens(l0_ref, dil[0]), tokens(l1_ref, dil[1]), tokens(l2_ref, dil[2])
    g0, g1, g2 = tokens(g0_ref, dil[0]), tokens(g1_ref, dil[1]), tokens(g2_ref, dil[2])
    mx = jnp.maximum(jnp.maximum(l0, l1), l2)
    e0, e1, e2 = jnp.exp(l0 - mx), jnp.exp(l1 - mx), jnp.exp(l2 - mx)
    oc = (e0 * g0 + e1 * g1 + e2 * g2) / (e0 + e1 + e2)
    mix = (_dot(oa_ref[...], w_ref[0:A_Q, :])
           + _dot(ob_ref[...], w_ref[A_Q:A_Q + B_V, :])
           + _dot(oc.astype(BF16), w_ref[A_Q + B_V:, :]))
    x = x_ref[...] + mix
    xo_ref[...] = x
    h_ref[...] = _rms(x, ln_ref[...], EPS).astype(BF16)


def _mix_out(x, oa, ob, og, lse, w_out, ln2, B, S):
    T = x.shape[0]
    tm = min(512, S)
    nt = S // tm
    row = lambda i: (i, 0)
    const = lambda i: (0, 0)
    grp = [pl.BlockSpec((None, d, tm // d, HEADS_W), lambda i: (i // nt, 0, i % nt, 0)) for _, d in C_PAIRS]
    n_slabs = 2 * sum(1 for _, d in C_PAIRS if d > 1)
    return pl.pallas_call(
        functools.partial(_mixout_kernel, tm=tm),
        grid=(T // tm,),
        in_specs=[pl.BlockSpec((tm, D_MODEL), row),
                  pl.BlockSpec((tm, A_Q), row),
                  pl.BlockSpec((tm, B_V), row)]
                 + grp + grp
                 + [pl.BlockSpec((D_MODEL, D_MODEL), const),
                    pl.BlockSpec((1, D_MODEL), const)],
        out_specs=[pl.BlockSpec((tm, D_MODEL), row), pl.BlockSpec((tm, D_MODEL), row)],
        out_shape=[jax.ShapeDtypeStruct((T, D_MODEL), F32), jax.ShapeDtypeStruct((T, D_MODEL), BF16)],
        scratch_shapes=[pltpu.VMEM((tm, LANES), F32)] * (n_slabs * (HEADS_W // LANES)),
        compiler_params=pltpu.CompilerParams(dimension_semantics=("parallel",),
                                             vmem_limit_bytes=VMEM_LIMIT_BYTES),
    )(x, oa, ob, *og, *lse, w_out, ln2.reshape(1, D_MODEL))


_UP_CHUNK = 256
_UP_HALO = 16


def _upgate_kernel(hp_ref, hc_ref, hn_ref, w_ref, cw_ref, cb_ref, o_ref, hh_ref, *, tm, seq_len):
    pos0 = (pl.program_id(0) * tm) % seq_len
    keep_prev = jnp.where(pos0 == 0, 0.0, 1.0)
    keep_next = jnp.where(pos0 + tm == seq_len, 0.0, 1.0)
    ext = tm + 2 * _UP_HALO
    hh_ref[0:_UP_HALO, :] = (hp_ref[...].astype(F32) * keep_prev).astype(BF16)
    hh_ref[_UP_HALO:_UP_HALO + tm, :] = hc_ref[...]
    hh_ref[_UP_HALO + tm:ext, :] = (hn_ref[...].astype(F32) * keep_next).astype(BF16)
    hh = hh_ref[...]

    def conv(c0):
        cols = slice(c0, c0 + _UP_CHUNK)
        u = _dot(hh, w_ref[:, cols])
        return (pltpu.roll(u, 1, axis=0) * cw_ref[0:1, cols] + u * cw_ref[1:2, cols]
                + pltpu.roll(u, ext - 1, axis=0) * cw_ref[2:3, cols] + cb_ref[:, cols])

    for c0 in range(0, D_FF, _UP_CHUNK):
        gate = conv(c0)
        val = conv(D_FF + c0)
        half = 0.5 * gate
        act = (half + half * jnp.tanh(half)) * val
        o_ref[:, c0:c0 + _UP_CHUNK] = act[_UP_HALO:_UP_HALO + tm, :].astype(BF16)


def _up_gate(h, w_up, conv_w, conv_b, *, seq_len):
    T = h.shape[0]
    tm = min(512, T)
    nh = tm // _UP_HALO
    n_halo = T // _UP_HALO
    row = lambda i: (i, 0)
    const = lambda i: (0, 0)
    return pl.pallas_call(
        functools.partial(_upgate_kernel, tm=tm, seq_len=seq_len),
        grid=(T // tm,),
        in_specs=[pl.BlockSpec((_UP_HALO, D_MODEL), lambda i: (jnp.maximum(i * nh - 1, 0), 0)),
                  pl.BlockSpec((tm, D_MODEL), row),
                  pl.BlockSpec((_UP_HALO, D_MODEL), lambda i: (jnp.minimum((i + 1) * nh, n_halo - 1), 0)),
                  pl.BlockSpec((D_MODEL, 2 * D_FF), const),
                  pl.BlockSpec((3, 2 * D_FF), const),
                  pl.BlockSpec((1, 2 * D_FF), const)],
        out_specs=pl.BlockSpec((tm, D_FF), row),
        out_shape=jax.ShapeDtypeStruct((T, D_FF), BF16),
        scratch_shapes=[pltpu.VMEM((tm + 2 * _UP_HALO, D_MODEL), BF16)],
        compiler_params=pltpu.CompilerParams(dimension_semantics=("parallel",),
                                             vmem_limit_bytes=VMEM_LIMIT_BYTES),
    )(h, h, h, w_up, conv_w, conv_b.reshape(1, 2 * D_FF))


def _down_kernel(a_ref, w_ref, x_ref, lnf_ref, o_ref, *, final_norm):
    x = x_ref[...] + _dot(a_ref[...], w_ref[...])
    if final_norm:
        x = _rms(x, lnf_ref[...], EPS)
    o_ref[...] = x


def _down(act, w_down, x, ln_f, *, final_norm):
    T = x.shape[0]
    tm = min(512, T)
    row = lambda i: (i, 0)
    const = lambda i: (0, 0)
    return pl.pallas_call(
        functools.partial(_down_kernel, final_norm=final_norm),
        grid=(T // tm,),
        in_specs=[pl.BlockSpec((tm, D_FF), row),
                  pl.BlockSpec((D_FF, D_MODEL), const),
                  pl.BlockSpec((tm, D_MODEL), row),
                  pl.BlockSpec((1, D_MODEL), const)],
        out_specs=pl.BlockSpec((tm, D_MODEL), row),
        out_shape=jax.ShapeDtypeStruct((T, D_MODEL), F32),
        compiler_params=pltpu.CompilerParams(dimension_semantics=("parallel",),
                                             vmem_limit_bytes=VMEM_LIMIT_BYTES),
    )(act, w_down, x, ln_f.reshape(1, D_MODEL))


def _trunk(x, p):
    B, S = x.shape[0], x.shape[1]
    xf = x.reshape(B * S, D_MODEL)
    a_slopes = _alibi_slopes(A_HEADS)
    c_slopes = _alibi_slopes(C_GROUPS * C_HEADS)
    for l in range(DEPTH):
        aq, ak, av, bq, bk, bv, *c = _in_proj(xf, p["ln1"][l], p["w_in"][l], B, S)
        as_seq = lambda t: t.reshape(B, 1, S, HEADS_W)
        (oa,) = _banded_attention(as_seq(aq), as_seq(ak), as_seq(av), window=A_WINDOW, slopes_eff=a_slopes,
                                  sink=p["a_sink"][l])
        lam_init = 0.8 - 0.6 * math.exp(-0.3 * l)
        ob = _diff_attention(bq, bk, bv, p["lam"][l], p["subln"][l], B=B, S=S, lam_init=lam_init)
        og, lse = [], []
        for g, (w, d) in enumerate(C_PAIRS):
            o_g, lse_g = _banded_attention(
                c[g], c[3 + g], c[6 + g], window=w // (2 * d),
                slopes_eff=[s * d for s in c_slopes[g * C_HEADS:(g + 1) * C_HEADS]], emit_lse=True)
            og.append(o_g)
            lse.append(lse_g)
        xf, h2 = _mix_out(xf, oa.reshape(B * S, HEADS_W), ob, og, lse, p["w_out"][l], p["ln2"][l], B, S)
        act = _up_gate(h2, p["w_up"][l], p["conv_w"][l], p["conv_b"][l], seq_len=S)
        xf = _down(act, p["w_down"][l], xf, p["ln_f"], final_norm=(l == DEPTH - 1))
    return xf.reshape(B, S, D_MODEL)


def kernel(x_prompt, x_sample, ln1, w_in, a_sink, lam_q1, lam_k1, lam_q2, lam_k2, subln, w_out, ln2, w_up,
           conv_w, conv_b, w_down, ln_f):
    p = {
        "ln1": ln1,
        "w_in": [_prep_w_in(w_in[l]) for l in range(DEPTH)],
        "a_sink": a_sink,
        "lam": jnp.stack([lam_q1, lam_k1, lam_q2, lam_k2], axis=1),
        "subln": subln,
        "w_out": w_out.astype(BF16),
        "ln2": ln2,
        "w_up": w_up.astype(BF16),
        "conv_w": conv_w,
        "conv_b": conv_b,
        "w_down": w_down.astype(BF16),
        "ln_f": ln_f,
    }
    return (_trunk(x_prompt, p), _trunk(x_sample, p))
```

```python
import functools
import itertools
import math

import numpy as np
import jax
import jax.numpy as jnp
from jax import lax
from jax.experimental import pallas as pl
from jax.experimental.pallas import tpu as pltpu

F32 = jnp.float32
BF16 = jnp.bfloat16

D_MODEL = 1024
DEPTH = 2
HEAD_DIM = 64
BLOCK = 128
EPS = 1e-6
SUBLN_EPS = 1e-5
NEG = -1e30
SCALE = HEAD_DIM ** -0.5
A_HEADS = 4
A_KV_HEADS = 2
A_WINDOW = 128
B_HEADS = 4
B_VDIM = 2 * HEAD_DIM
C_PAIRS = ((128, 1), (512, 4), (2048, 16))
C_GROUPS = 3
C_HEADS = 4
D_FF = 2816
A_Q = A_HEADS * HEAD_DIM
A_KV = A_KV_HEADS * HEAD_DIM
B_QK = B_HEADS * 2 * HEAD_DIM
B_V = B_HEADS * B_VDIM
C_QKV = C_GROUPS * C_HEADS * HEAD_DIM
HEADS_W = 4 * HEAD_DIM

VMEM_LIMIT_BYTES = 56 * 1024 * 1024
LANES = 128
SUBLANES = 8


def _alibi_slopes(n):
    return [2.0 ** (-8.0 * k / n) for k in range(1, n + 1)]


LOG2E = math.log2(math.e)
_AUG_SPLITS = 3


def _bf16_round(x):
    b = np.asarray(x, np.float32).view(np.uint32)
    return ((b + (((b >> 16) & 1) + 0x7FFF)) & 0xFFFF0000).astype(np.uint32).view(np.float32)


def _diff_alibi_tables():
    c = np.asarray(_alibi_slopes(B_HEADS), np.float64) * LOG2E
    pieces, rem = [], c.copy()
    for _ in range(_AUG_SPLITS):
        pk = _bf16_round(rem).astype(np.float64)
        pieces.append(pk)
        rem = rem - pk
    pieces = np.stack(pieces, axis=1)
    ktab = np.zeros((B_HEADS, LANES), np.float32)
    qtab = np.zeros((B_HEADS, 1, LANES), np.float32)
    ktab[:, 0:3] = pieces
    ktab[:, 3:6] = pieces
    qtab[:, 0, 6:9] = pieces
    qtab[:, 0, 9:12] = pieces
    cfull = np.broadcast_to(c.astype(np.float32)[:, None, None], (B_HEADS, 1, LANES)).copy()
    return jnp.asarray(ktab), jnp.asarray(qtab), jnp.asarray(cfull)


def _dot(a, b):
    return jnp.dot(a, b, preferred_element_type=F32)


def _dot_nt(a, b):
    return lax.dot_general(a, b, (((1,), (1,)), ((), ())), preferred_element_type=F32)


def _rms(x, g, eps):
    return x * lax.rsqrt(jnp.mean(x * x, axis=-1, keepdims=True) + eps) * g


_W_AQ, _W_AK, _W_AV = 0, 256, 512
_W_BQ, _W_BK, _W_BV = 768, 1280, 1792
_W_CQ, _W_CK, _W_CV = 2304, 3072, 3840
_W_COLS = 4608


def _inproj_kernel(x_ref, g_ref, w_ref, ktab_ref, aq_ref, ak_ref, av_ref, bq_ref, bk_ref, bv_ref, *rest, seq_len, tm):
    c_refs, slabs = rest[:9], rest[9:]
    h = _rms(x_ref[...], g_ref[...], EPS).astype(BF16)

    def proj(c0, c1):
        return _dot(h, w_ref[:, c0:c1])

    aq_ref[...] = (proj(_W_AQ, _W_AK) * SCALE).astype(BF16)
    ak_ref[...] = proj(_W_AK, _W_AV).astype(BF16)
    av_ref[...] = proj(_W_AV, _W_BQ).astype(BF16)
    bq_ref[...] = (proj(_W_BQ, _W_BK) * (SCALE * LOG2E)).astype(BF16)
    pos = (pl.program_id(0) * tm + lax.broadcasted_iota(jnp.int32, (tm, LANES), 0)) % seq_len
    lane = lax.broadcasted_iota(jnp.int32, (tm, LANES), 1)
    pos_cols = jnp.where(lane < 9, ((pos >> 7) * 128).astype(F32),
                         jnp.where(lane < 12, (pos & 127).astype(F32), 0.0))
    ones = jnp.ones((tm, LANES), BF16)
    bk = proj(_W_BK, _W_BV).astype(BF16)
    bv = proj(_W_BV, _W_CQ).astype(BF16)
    for hh in range(B_HEADS):
        bk_ref[hh, :, 0:LANES] = bk[:, hh * LANES:(hh + 1) * LANES]
        bk_ref[hh, :, LANES:2 * LANES] = jnp.where(lane < 6, ktab_ref[hh:hh + 1, :], pos_cols).astype(BF16)
        bv_ref[hh, :, 0:LANES] = bv[:, hh * LANES:(hh + 1) * LANES]
        bv_ref[hh, :, LANES:2 * LANES] = ones
    slab = 0
    for part, (c0, scale) in enumerate(((_W_CQ, SCALE), (_W_CK, 1.0), (_W_CV, 1.0))):
        pc = proj(c0, c0 + C_QKV) * scale
        for g, (_, d) in enumerate(C_PAIRS):
            dst = c_refs[3 * part + g]
            val = pc[:, g * HEADS_W:(g + 1) * HEADS_W]
            if d == 1:
                dst[0] = val.astype(BF16)
            else:
                for c in range(HEADS_W // LANES):
                    buf = slabs[slab]
                    slab += 1
                    lanes = slice(c * LANES, (c + 1) * LANES)
                    buf[...] = val[:, lanes]
                    for r in range(d):
                        dst[r, :, lanes] = buf[pl.ds(r, tm // d, stride=d), :].astype(BF16)


def _in_proj(x, g, w, B, S):
    T = x.shape[0]
    tm = min(512, S)
    nt = S // tm
    row = lambda i: (i, 0)
    const = lambda i: (0, 0)
    outs = [
        (jax.ShapeDtypeStruct((T, 256), BF16), pl.BlockSpec((tm, 256), row)),
        (jax.ShapeDtypeStruct((T, 256), BF16), pl.BlockSpec((tm, 256), row)),
        (jax.ShapeDtypeStruct((T, 256), BF16), pl.BlockSpec((tm, 256), row)),
        (jax.ShapeDtypeStruct((T, B_QK), BF16), pl.BlockSpec((tm, B_QK), row)),
        (jax.ShapeDtypeStruct((B_HEADS, T, 256), BF16), pl.BlockSpec((B_HEADS, tm, 256), lambda i: (0, i, 0))),
        (jax.ShapeDtypeStruct((B_HEADS, T, 256), BF16), pl.BlockSpec((B_HEADS, tm, 256), lambda i: (0, i, 0))),
    ]
    for _ in range(3):
        for _, d in C_PAIRS:
            outs.append((jax.ShapeDtypeStruct((B, d, S // d, HEADS_W), BF16),
                         pl.BlockSpec((None, d, tm // d, HEADS_W), lambda i: (i // nt, 0, i % nt, 0))))
    n_slabs = 3 * sum(1 for _, d in C_PAIRS if d > 1)
    return pl.pallas_call(
        functools.partial(_inproj_kernel, seq_len=S, tm=tm),
        grid=(T // tm,),
        in_specs=[pl.BlockSpec((tm, D_MODEL), row),
                  pl.BlockSpec((1, D_MODEL), const),
                  pl.BlockSpec((D_MODEL, _W_COLS), const),
                  pl.BlockSpec((B_HEADS, LANES), const)],
        out_specs=[o[1] for o in outs],
        out_shape=[o[0] for o in outs],
        scratch_shapes=[pltpu.VMEM((tm, LANES), F32)] * (n_slabs * (HEADS_W // LANES)),
        compiler_params=pltpu.CompilerParams(dimension_semantics=("parallel",),
                                             vmem_limit_bytes=VMEM_LIMIT_BYTES),
    )(x, g.reshape(1, D_MODEL), w, _diff_alibi_tables()[0])


def _prep_w_in(w):
    g = A_HEADS // A_KV_HEADS
    aq = w[:, :A_Q]
    ak = jnp.repeat(w[:, A_Q:A_Q + A_KV].reshape(D_MODEL, A_KV_HEADS, 1, HEAD_DIM), g, axis=2).reshape(D_MODEL, A_Q)
    av = jnp.repeat(w[:, A_Q + A_KV:A_Q + 2 * A_KV].reshape(D_MODEL, A_KV_HEADS, 1, HEAD_DIM), g, axis=2).reshape(D_MODEL, A_Q)
    return jnp.concatenate([aq, ak, av, w[:, A_Q + 2 * A_KV:]], axis=1).astype(BF16)


_BAND_SUBBLOCKS = 16


def _banded_kernel(*refs, rr, tq, koff, wband, halo, has_sink, emit_lse):
    q_ref, kp_ref, kc_ref, kn_ref, vp_ref, vc_ref, vn_ref, bias_ref, hmask_ref = refs[:9]
    rest = refs[9:]
    if has_sink:
        sink_ref, rest = rest[0], rest[1:]
    o_ref = rest[0]
    lse_ref = rest[1] if emit_lse else None

    first = pl.program_id(2) == 0
    last = pl.program_id(2) == pl.num_programs(2) - 1
    col = lax.broadcasted_iota(jnp.int32, (1, wband), 1)
    lane_head = lax.broadcasted_iota(jnp.int32, (BLOCK, HEADS_W), 1) // HEAD_DIM
    if has_sink:
        sink_col = lax.broadcasted_iota(jnp.int32, (1, LANES), 1) == 0

        def in_col0(x, new):
            return jnp.concatenate([jnp.where(sink_col, new, x[:, :LANES]), x[:, LANES:]], axis=1)
    for r, i in itertools.product(range(rr), range(tq // BLOCK)):
        lo = BLOCK * i - koff
        hi = lo + wband

        def band(p_ref, c_ref, n_ref):
            parts = []
            if lo < 0:
                parts.append(p_ref[r, halo + lo:halo, :])
            parts.append(c_ref[r, max(lo, 0):min(hi, tq), :])
            if hi > tq:
                parts.append(n_ref[r, 0:hi - tq, :])
            return parts[0] if len(parts) == 1 else jnp.concatenate(parts, axis=0)

        kb = band(kp_ref, kc_ref, kn_ref)
        vb = band(vp_ref, vc_ref, vn_ref)
        q = q_ref[r, BLOCK * i:BLOCK * (i + 1), :]
        qs = jnp.concatenate([q] * 4, axis=0) * hmask_ref[...]
        s = _dot_nt(qs, kb) + bias_ref[...]
        if lo < 0:
            s = s + jnp.where(first & (col < -lo), NEG, 0.0)
        if hi > tq:
            s = s + jnp.where(last & (col >= wband - (hi - tq)), NEG, 0.0)
        if has_sink:
            s = in_col0(s, sink_ref[...])
        m = jnp.max(s, axis=-1, keepdims=True)
        p = jnp.exp(s - m)
        l = jnp.sum(p, axis=-1, keepdims=True)
        if has_sink:
            p = in_col0(p, 0.0)
        pv = _dot(p.astype(BF16), vb) / l
        o = jnp.zeros((BLOCK, HEADS_W), F32)
        lse_o = jnp.zeros((BLOCK, HEADS_W), F32)
        lse = m + jnp.log(l) if emit_lse else None
        for h in range(4):
            hr = slice(BLOCK * h, BLOCK * (h + 1))
            o = jnp.where(lane_head == h, pv[hr], o)
            if emit_lse:
                lse_o = jnp.where(lane_head == h, lse[hr], lse_o)
        o_ref[r, BLOCK * i:BLOCK * (i + 1), :] = o.astype(BF16)
        if emit_lse:
            lse_ref[r, BLOCK * i:BLOCK * (i + 1), :] = lse_o


def _band_tables(window, koff, wband, slopes_eff):
    il = np.arange(BLOCK)[:, None]
    c = np.arange(wband)[None, :]
    dist = np.abs(c - koff - il).astype(np.float64)
    bias = np.concatenate([np.where(dist <= window, -s * dist, NEG) for s in slopes_eff], axis=0)
    hmask = np.concatenate([np.broadcast_to((np.arange(HEADS_W) // HEAD_DIM == h)[None, :], (BLOCK, HEADS_W))
                            for h in range(4)], axis=0)
    return jnp.asarray(bias, dtype=F32), jnp.asarray(hmask, dtype=BF16)


def _banded_attention(q, k, v, *, window, slopes_eff, sink=None, emit_lse=False):
    B, d, Sd, _ = q.shape
    tq = min(_BAND_SUBBLOCKS * BLOCK, Sd)
    rr = min(d, max(1, _BAND_SUBBLOCKS * BLOCK // tq))
    koff, wband, halo = (BLOCK // 2, 2 * BLOCK, BLOCK) if window <= BLOCK // 2 else (3 * BLOCK // 2, 4 * BLOCK, 2 * BLOCK)
    halo = min(halo, tq)
    assert koff <= halo and wband - koff - BLOCK <= halo and window <= koff and window <= wband - koff - BLOCK
    nh, nhb = tq // halo, Sd // halo
    bias, hmask = _band_tables(window, koff, wband, slopes_eff)
    cur = pl.BlockSpec((None, rr, tq, HEADS_W), lambda b, r, i: (b, r, i, 0))
    prev = pl.BlockSpec((None, rr, halo, HEADS_W), lambda b, r, i: (b, r, jnp.maximum(i * nh - 1, 0), 0))
    nxt = pl.BlockSpec((None, rr, halo, HEADS_W), lambda b, r, i: (b, r, jnp.minimum((i + 1) * nh, nhb - 1), 0))
    const = lambda b, r, i: (0, 0)
    in_specs = [cur, prev, cur, nxt, prev, cur, nxt,
                pl.BlockSpec((4 * BLOCK, wband), const), pl.BlockSpec((4 * BLOCK, HEADS_W), const)]
    args = [q, k, k, k, v, v, v, bias, hmask]
    if sink is not None:
        assert koff > window
        in_specs.append(pl.BlockSpec((4 * BLOCK, LANES), const))
        args.append(jnp.broadcast_to(jnp.repeat(sink.astype(F32), BLOCK)[:, None], (4 * BLOCK, LANES)))
    out_shape = [jax.ShapeDtypeStruct((B, d, Sd, HEADS_W), BF16)]
    out_specs = [cur]
    if emit_lse:
        out_shape.append(jax.ShapeDtypeStruct((B, d, Sd, HEADS_W), F32))
        out_specs.append(cur)
    return pl.pallas_call(
        functools.partial(_banded_kernel, rr=rr, tq=tq, koff=koff, wband=wband, halo=halo, has_sink=sink is not None,
                          emit_lse=emit_lse),
        grid=(B, d // rr, Sd // tq),
        in_specs=in_specs,
        out_specs=out_specs,
        out_shape=out_shape,
        compiler_params=pltpu.CompilerParams(dimension_semantics=("parallel", "parallel", "parallel"),
                                             vmem_limit_bytes=VMEM_LIMIT_BYTES),
    )(*args)


_SOFTMAX_ROWS = 128
_PIPE_SLOTS = 4


def _diff_kernel(coef_ref, qtab_ref, lam_ref, subln_ref, q_ref, k_ref, v_ref, o_ref,
                 qs_ref, m_ref, acc_ref, *bufs, tq, nk, lam_init):
    qi = pl.program_id(2)
    s_refs, p_refs, a_refs = bufs[0:4], bufs[4:8], bufs[8:12]

    q = q_ref[...].astype(F32)
    lane = lax.broadcasted_iota(jnp.int32, (tq, LANES), 1)
    pos = qi * tq + lax.broadcasted_iota(jnp.int32, (tq, LANES), 0)
    base = jnp.where(lane < 3, -((pos >> 7) * 128).astype(F32),
                     jnp.where(lane < 6, -(pos & 127).astype(F32), qtab_ref[...]))
    q1 = jnp.where(lane < HEAD_DIM, q, 0.0).astype(BF16)
    q2 = jnp.where(lane >= HEAD_DIM, q, 0.0).astype(BF16)
    for var, sign in enumerate((1.0, -1.0, 0.0)):
        aug = (sign * base).astype(BF16)
        qs_ref[var, 0:tq, 0:LANES] = q1
        qs_ref[var, 0:tq, LANES:2 * LANES] = aug
        qs_ref[var, tq:2 * tq, 0:LANES] = q2
        qs_ref[var, tq:2 * tq, LANES:2 * LANES] = aug
    m_ref[...] = jnp.full(m_ref.shape, NEG, F32)
    acc_ref[...] = jnp.zeros(acc_ref.shape, F32)

    def key_tile(t):
        return jnp.where(t == 0, qi, jnp.where(t <= qi, t - 1, t))

    def rows_of(j):
        return pl.ds(pl.multiple_of(j * tq, tq), tq)

    def stage1_diag(slot):
        s = _dot_nt(qs_ref[2], k_ref[rows_of(qi), :])
        r = lax.broadcasted_iota(jnp.int32, s.shape, 0) & (tq - 1)
        c = lax.broadcasted_iota(jnp.int32, s.shape, 1)
        s_refs[slot][...] = s - coef_ref[:, 0:1] * jnp.abs(r - c).astype(F32)

    def stage1(t, slot):
        j = key_tile(t)
        s_refs[slot][...] = _dot_nt(qs_ref[jnp.where(j < qi, 0, 1)], k_ref[rows_of(j), :])

    def stage2(slot):
        s_a, s_b = s_refs[slot], s_refs[slot + 1]
        for r0 in range(0, 2 * tq, _SOFTMAX_ROWS):
            rows = slice(r0, r0 + _SOFTMAX_ROWS)
            m_prev = m_ref[rows, :]
            row_max = jnp.max(jnp.maximum(s_a[rows, :], s_b[rows, :]), axis=-1, keepdims=True)
            m_new = jnp.maximum(m_prev, row_max)
            a_refs[slot][rows, :] = jnp.exp2(m_prev - m_new)
            m_ref[rows, :] = m_new
        for r0 in range(0, 2 * tq, _SOFTMAX_ROWS):
            rows = slice(r0, r0 + _SOFTMAX_ROWS)
            m_new = jnp.concatenate([m_ref[rows, :]] * (tq // LANES), axis=1)
            p_refs[slot][rows, :] = jnp.exp2(s_a[rows, :] - m_new).astype(BF16)
            p_refs[slot + 1][rows, :] = jnp.exp2(s_b[rows, :] - m_new).astype(BF16)

    def stage3(j_a, j_b, slot):
        pv = _dot(p_refs[slot][...], v_ref[rows_of(j_a), :]) + _dot(p_refs[slot + 1][...], v_ref[rows_of(j_b), :])
        a = a_refs[slot][...]
        acc_ref[...] = acc_ref[...] * jnp.concatenate([a, a], axis=1) + pv

    def step(u, first, has1=True, has2=True, has3=True):
        other = 2 - first
        if has2:
            stage2(other)
        if has1:
            stage1(2 * u, first)
            stage1(2 * u + 1, first + 1)
        if has3:
            stage3(key_tile(2 * u - 4), key_tile(2 * u - 3), first)

    stage1_diag(0)
    stage1(1, 1)
    step(1, 2, has3=False)

    for u in range(2, nk // 2):
        step(u, 2 * (u % 2))
    step(nk // 2, 0, has1=False)
    step(nk // 2 + 1, 2, has1=False, has2=False)

    acc = acc_ref[...]
    o1 = acc[:tq, :LANES] / acc[:tq, LANES:]
    o2 = acc[tq:, :LANES] / acc[tq:, LANES:]
    lv = lam_ref[...]
    lam = (jnp.exp(jnp.sum(lv[0:1] * lv[1:2], axis=-1, keepdims=True))
           - jnp.exp(jnp.sum(lv[2:3] * lv[3:4], axis=-1, keepdims=True)) + lam_init)
    o = _rms(o1 - lam * o2, subln_ref[...], SUBLN_EPS) * (1.0 - lam_init)
    o_ref[...] = o.astype(BF16)


def _diff_attention(q, k, v, lam_vecs, subln, *, B, S, lam_init):
    tq = min(512, S // 4)
    nq = S // tq
    assert nq % 4 == 0
    _, qtab, coef = _diff_alibi_tables()
    qv = q.reshape(B, S, B_QK)
    kv = k.reshape(B_HEADS, B, S, 2 * LANES)
    vv = v.reshape(B_HEADS, B, S, 2 * LANES)
    out = pl.pallas_call(
        functools.partial(_diff_kernel, tq=tq, nk=nq, lam_init=lam_init),
        grid=(B, B_HEADS, nq),
        in_specs=[pl.BlockSpec((None, 1, LANES), lambda b, h, i: (h, 0, 0)),
                  pl.BlockSpec((None, 1, LANES), lambda b, h, i: (h, 0, 0)),
                  pl.BlockSpec((4, HEAD_DIM), lambda b, h, i: (0, 0)),
                  pl.BlockSpec((1, B_VDIM), lambda b, h, i: (0, 0)),
                  pl.BlockSpec((None, tq, LANES), lambda b, h, i: (b, i, h)),
                  pl.BlockSpec((None, None, S, 2 * LANES), lambda b, h, i: (h, b, 0, 0)),
                  pl.BlockSpec((None, None, S, 2 * LANES), lambda b, h, i: (h, b, 0, 0))],
        out_specs=pl.BlockSpec((None, tq, LANES), lambda b, h, i: (b, i, h)),
        out_shape=jax.ShapeDtypeStruct((B, S, B_V), BF16),
        scratch_shapes=[pltpu.VMEM((3, 2 * tq, 2 * LANES), BF16),
                        pltpu.VMEM((2 * tq, LANES), F32),
                        pltpu.VMEM((2 * tq, 2 * LANES), F32),
                        *[pltpu.VMEM((2 * tq, tq), F32)] * _PIPE_SLOTS,
                        *[pltpu.VMEM((2 * tq, tq), BF16)] * _PIPE_SLOTS,
                        *[pltpu.VMEM((2 * tq, LANES), F32)] * _PIPE_SLOTS],
        compiler_params=pltpu.CompilerParams(
            dimension_semantics=("parallel", "parallel", "arbitrary"),
            vmem_limit_bytes=VMEM_LIMIT_BYTES),
    )(coef, qtab, lam_vecs.astype(F32), subln.reshape(1, B_VDIM).astype(F32), qv, kv, vv)
    return out.reshape(B * S, B_V)


def _mixout_kernel(x_ref, oa_ref, ob_ref, g0_ref, g1_ref, g2_ref, l0_ref, l1_ref, l2_ref, w_ref, ln_ref,
                   xo_ref, h_ref, *slabs, tm):
    slabs = list(slabs)

    def tokens(ref, d):
        if d == 1:
            return ref[0].astype(F32)
        halves = []
        for c in range(HEADS_W // LANES):
            buf = slabs.pop()
            for r in range(d):
                buf[pl.ds(r, tm // d, stride=d), :] = ref[r, :, c * LANES:(c + 1) * LANES].astype(F32)
            halves.append(buf[...])
        return jnp.concatenate(halves, axis=1)

    dil = [d for _, d in C_PAIRS]
    l0, l1, l2 = tokens(l0_ref, dil[0]), tokens(l1_ref, dil[1]), tokens(l2_ref, dil[2])
    g0, g1, g2 = tokens(g0_ref, dil[0]), tokens(g1_ref, dil[1]), tokens(g2_ref, dil[2])
    mx = jnp.maximum(jnp.maximum(l0, l1), l2)
    e0, e1, e2 = jnp.exp(l0 - mx), jnp.exp(l1 - mx), jnp.exp(l2 - mx)
    oc = (e0 * g0 + e1 * g1 + e2 * g2) / (e0 + e1 + e2)
    mix = (_dot(oa_ref[...], w_ref[0:A_Q, :])
           + _dot(ob_ref[...], w_ref[A_Q:A_Q + B_V, :])
           + _dot(oc.astype(BF16), w_ref[A_Q + B_V:, :]))
    x = x_ref[...] + mix
    xo_ref[...] = x
    h_ref[...] = _rms(x, ln_ref[...], EPS).astype(BF16)


def _mix_out(x, oa, ob, og, lse, w_out, ln2, B, S):
    T = x.shape[0]
    tm = min(512, S)
    nt = S // tm
    row = lambda i: (i, 0)
    const = lambda i: (0, 0)
    grp = [pl.BlockSpec((None, d, tm // d, HEADS_W), lambda i: (i // nt, 0, i % nt, 0)) for _, d in C_PAIRS]
    n_slabs = 2 * sum(1 for _, d in C_PAIRS if d > 1)
    return pl.pallas_call(
        functools.partial(_mixout_kernel, tm=tm),
        grid=(T // tm,),
        in_specs=[pl.BlockSpec((tm, D_MODEL), row),
                  pl.BlockSpec((tm, A_Q), row),
                  pl.BlockSpec((tm, B_V), row)]
                 + grp + grp
                 + [pl.BlockSpec((D_MODEL, D_MODEL), const),
                    pl.BlockSpec((1, D_MODEL), const)],
        out_specs=[pl.BlockSpec((tm, D_MODEL), row), pl.BlockSpec((tm, D_MODEL), row)],
        out_shape=[jax.ShapeDtypeStruct((T, D_MODEL), F32), jax.ShapeDtypeStruct((T, D_MODEL), BF16)],
        scratch_shapes=[pltpu.VMEM((tm, LANES), F32)] * (n_slabs * (HEADS_W // LANES)),
        compiler_params=pltpu.CompilerParams(dimension_semantics=("parallel",),
                                             vmem_limit_bytes=VMEM_LIMIT_BYTES),
    )(x, oa, ob, *og, *lse, w_out, ln2.reshape(1, D_MODEL))


_UP_CHUNK = 256
_UP_HALO = 16


def _upconv_kernel(hp_ref, hc_ref, hn_ref, w_ref, cw_ref, cb_ref, o_ref, hh_ref, *, tm, seq_len):
    pos0 = (pl.program_id(0) * tm) % seq_len
    keep_prev = jnp.where(pos0 == 0, 0.0, 1.0)
    keep_next = jnp.where(pos0 + tm == seq_len, 0.0, 1.0)
    ext = tm + 2 * _UP_HALO
    hh_ref[0:_UP_HALO, :] = (hp_ref[...].astype(F32) * keep_prev).astype(BF16)
    hh_ref[_UP_HALO:_UP_HALO + tm, :] = hc_ref[...]
    hh_ref[_UP_HALO + tm:ext, :] = (hn_ref[...].astype(F32) * keep_next).astype(BF16)
    hh = hh_ref[...]

    def conv(c0):
        cols = slice(c0, c0 + _UP_CHUNK)
        u = _dot(hh, w_ref[:, cols])
        return (pltpu.roll(u, 1, axis=0) * cw_ref[0:1, cols] + u * cw_ref[1:2, cols]
                + pltpu.roll(u, ext - 1, axis=0) * cw_ref[2:3, cols] + cb_ref[:, cols])

    for c0 in range(0, 2 * D_FF, _UP_CHUNK):
        o_ref[:, c0:c0 + _UP_CHUNK] = conv(c0)[_UP_HALO:_UP_HALO + tm, :].astype(BF16)


def _up_conv(h, w_up, conv_w, conv_b, *, seq_len):
    T = h.shape[0]
    tm = min(512, T)
    nh = tm // _UP_HALO
    n_halo = T // _UP_HALO
    row = lambda i: (i, 0)
    const = lambda i: (0, 0)
    return pl.pallas_call(
        functools.partial(_upconv_kernel, tm=tm, seq_len=seq_len),
        grid=(T // tm,),
        in_specs=[pl.BlockSpec((_UP_HALO, D_MODEL), lambda i: (jnp.maximum(i * nh - 1, 0), 0)),
                  pl.BlockSpec((tm, D_MODEL), row),
                  pl.BlockSpec((_UP_HALO, D_MODEL), lambda i: (jnp.minimum((i + 1) * nh, n_halo - 1), 0)),
                  pl.BlockSpec((D_MODEL, 2 * D_FF), const),
                  pl.BlockSpec((3, 2 * D_FF), const),
                  pl.BlockSpec((1, 2 * D_FF), const)],
        out_specs=pl.BlockSpec((tm, 2 * D_FF), row),
        out_shape=jax.ShapeDtypeStruct((T, 2 * D_FF), BF16),
        scratch_shapes=[pltpu.VMEM((tm + 2 * _UP_HALO, D_MODEL), BF16)],
        compiler_params=pltpu.CompilerParams(dimension_semantics=("parallel",),
                                             vmem_limit_bytes=VMEM_LIMIT_BYTES),
    )(h, h, h, w_up, conv_w, conv_b.reshape(1, 2 * D_FF))


def _down_kernel(c_ref, w_ref, x_ref, lnf_ref, o_ref, act_ref, *, final_norm):
    for c0 in range(0, D_FF, _UP_CHUNK):
        half = 0.5 * c_ref[:, c0:c0 + _UP_CHUNK].astype(F32)
        val = c_ref[:, D_FF + c0:D_FF + c0 + _UP_CHUNK].astype(F32)
        act_ref[:, c0:c0 + _UP_CHUNK] = ((half + half * jnp.tanh(half)) * val).astype(BF16)
    x = x_ref[...] + _dot(act_ref[...], w_ref[...])
    if final_norm:
        x = _rms(x, lnf_ref[...], EPS)
    o_ref[...] = x


def _down(c, w_down, x, ln_f, *, final_norm):
    T = x.shape[0]
    tm = min(512, T)
    row = lambda i: (i, 0)
    const = lambda i: (0, 0)
    return pl.pallas_call(
        functools.partial(_down_kernel, final_norm=final_norm),
        grid=(T // tm,),
        in_specs=[pl.BlockSpec((tm, 2 * D_FF), row),
                  pl.BlockSpec((D_FF, D_MODEL), const),
                  pl.BlockSpec((tm, D_MODEL), row),
                  pl.BlockSpec((1, D_MODEL), const)],
        out_specs=pl.BlockSpec((tm, D_MODEL), row),
        out_shape=jax.ShapeDtypeStruct((T, D_MODEL), F32),
        scratch_shapes=[pltpu.VMEM((tm, D_FF), BF16)],
        compiler_params=pltpu.CompilerParams(dimension_semantics=("parallel",),
                                             vmem_limit_bytes=VMEM_LIMIT_BYTES),
    )(c, w_down, x, ln_f.reshape(1, D_MODEL))


def _trunk(x, p):
    B, S = x.shape[0], x.shape[1]
    xf = x.reshape(B * S, D_MODEL)
    a_slopes = _alibi_slopes(A_HEADS)
    c_slopes = _alibi_slopes(C_GROUPS * C_HEADS)
    for l in range(DEPTH):
        aq, ak, av, bq, bk, bv, *c = _in_proj(xf, p["ln1"][l], p["w_in"][l], B, S)
        as_seq = lambda t: t.reshape(B, 1, S, HEADS_W)
        (oa,) = _banded_attention(as_seq(aq), as_seq(ak), as_seq(av), window=A_WINDOW, slopes_eff=a_slopes,
                                  sink=p["a_sink"][l])
        lam_init = 0.8 - 0.6 * math.exp(-0.3 * l)
        ob = _diff_attention(bq, bk, bv, p["lam"][l], p["subln"][l], B=B, S=S, lam_init=lam_init)
        og, lse = [], []
        for g, (w, d) in enumerate(C_PAIRS):
            o_g, lse_g = _banded_attention(
                c[g], c[3 + g], c[6 + g], window=w // (2 * d),
                slopes_eff=[s * d for s in c_slopes[g * C_HEADS:(g + 1) * C_HEADS]], emit_lse=True)
            og.append(o_g)
            lse.append(lse_g)
        xf, h2 = _mix_out(xf, oa.reshape(B * S, HEADS_W), ob, og, lse, p["w_out"][l], p["ln2"][l], B, S)
        c_mlp = _up_conv(h2, p["w_up"][l], p["conv_w"][l], p["conv_b"][l], seq_len=S)
        xf = _down(c_mlp, p["w_down"][l], xf, p["ln_f"], final_norm=(l == DEPTH - 1))
    return xf.reshape(B, S, D_MODEL)


def kernel(x_prompt, x_sample, ln1, w_in, a_sink, lam_q1, lam_k1, lam_q2, lam_k2, subln, w_out, ln2, w_up,
           conv_w, conv_b, w_down, ln_f):
    p = {
        "ln1": ln1,
        "w_in": [_prep_w_in(w_in[l]) for l in range(DEPTH)],
        "a_sink": a_sink,
        "lam": jnp.stack([lam_q1, lam_k1, lam_q2, lam_k2], axis=1),
        "subln": subln,
        "w_out": w_out.astype(BF16),
        "ln2": ln2,
        "w_up": w_up.astype(BF16),
        "conv_w": conv_w,
        "conv_b": conv_b,
        "w_down": w_down.astype(BF16),
        "ln_f": ln_f,
    }
    return (_trunk(x_prompt, p), _trunk(x_sample, p))
```

```python
import functools
import itertools
import math

import numpy as np
import jax
import jax.numpy as jnp
from jax import lax
from jax.experimental import pallas as pl
from jax.experimental.pallas import tpu as pltpu

F32 = jnp.float32
BF16 = jnp.bfloat16

D_MODEL = 1024
DEPTH = 2
HEAD_DIM = 64
BLOCK = 128
EPS = 1e-6
SUBLN_EPS = 1e-5
NEG = -1e30
SCALE = HEAD_DIM ** -0.5
A_HEADS = 4
A_KV_HEADS = 2
A_WINDOW = 128
B_HEADS = 4
B_VDIM = 2 * HEAD_DIM
C_PAIRS = ((128, 1), (512, 4), (2048, 16))
C_GROUPS = 3
C_HEADS = 4
D_FF = 2816
A_Q = A_HEADS * HEAD_DIM
A_KV = A_KV_HEADS * HEAD_DIM
B_QK = B_HEADS * 2 * HEAD_DIM
B_V = B_HEADS * B_VDIM
C_QKV = C_GROUPS * C_HEADS * HEAD_DIM
HEADS_W = 4 * HEAD_DIM

VMEM_LIMIT_BYTES = 56 * 1024 * 1024
LANES = 128
SUBLANES = 8


def _alibi_slopes(n):
    return [2.0 ** (-8.0 * k / n) for k in range(1, n + 1)]


LOG2E = math.log2(math.e)
_AUG_SPLITS = 3
_BLOCK_LOG2 = BLOCK.bit_length() - 1


def _bf16_round(x):
    b = np.asarray(x, np.float32).view(np.uint32)
    return ((b + (((b >> 16) & 1) + 0x7FFF)) & 0xFFFF0000).astype(np.uint32).view(np.float32)


def _diff_alibi_tables():
    c = np.asarray(_alibi_slopes(B_HEADS), np.float64) * LOG2E
    pieces, rem = [], c.copy()
    for _ in range(_AUG_SPLITS):
        pk = _bf16_round(rem).astype(np.float64)
        pieces.append(pk)
        rem = rem - pk
    pieces = np.stack(pieces, axis=1)
    ktab = np.zeros((B_HEADS, LANES), np.float32)
    qtab = np.zeros((B_HEADS, 1, LANES), np.float32)
    n = _AUG_SPLITS
    ktab[:, 0:n] = pieces
    ktab[:, n:2 * n] = pieces
    qtab[:, 0, 2 * n:3 * n] = pieces
    qtab[:, 0, 3 * n:4 * n] = pieces
    cfull = np.broadcast_to(c.astype(np.float32)[:, None, None], (B_HEADS, 1, LANES)).copy()
    return jnp.asarray(ktab), jnp.asarray(qtab), jnp.asarray(cfull)


def _dot(a, b):
    return jnp.dot(a, b, preferred_element_type=F32)


def _dot_nt(a, b):
    return lax.dot_general(a, b, (((1,), (1,)), ((), ())), preferred_element_type=F32)


def _rms(x, g, eps):
    return x * lax.rsqrt(jnp.mean(x * x, axis=-1, keepdims=True) + eps) * g


_W_AQ, _W_AK, _W_AV = 0, 256, 512
_W_BQ, _W_BK, _W_BV = 768, 1280, 1792
_W_CQ, _W_CK, _W_CV = 2304, 3072, 3840
_W_COLS = 4608


def _inproj_kernel(x_ref, g_ref, w_ref, ktab_ref, aq_ref, ak_ref, av_ref, bq_ref, bk_ref, bv_ref, *rest, seq_len, tm):
    c_refs, slabs = rest[:9], rest[9:]
    h = _rms(x_ref[...], g_ref[...], EPS).astype(BF16)

    def proj(c0, c1):
        return _dot(h, w_ref[:, c0:c1])

    aq_ref[...] = (proj(_W_AQ, _W_AK) * SCALE).astype(BF16)
    ak_ref[...] = proj(_W_AK, _W_AV).astype(BF16)
    av_ref[...] = proj(_W_AV, _W_BQ).astype(BF16)
    bq_ref[...] = (proj(_W_BQ, _W_BK) * (SCALE * LOG2E)).astype(BF16)
    pos = (pl.program_id(0) * tm + lax.broadcasted_iota(jnp.int32, (tm, LANES), 0)) % seq_len
    lane = lax.broadcasted_iota(jnp.int32, (tm, LANES), 1)
    pos_cols = jnp.where(lane < 3 * _AUG_SPLITS, ((pos >> _BLOCK_LOG2) << _BLOCK_LOG2).astype(F32),
                         jnp.where(lane < 4 * _AUG_SPLITS, (pos & (BLOCK - 1)).astype(F32), 0.0))
    ones = jnp.ones((tm, LANES), BF16)
    bk = proj(_W_BK, _W_BV)
    bv = proj(_W_BV, _W_CQ).astype(BF16)
    for hh in range(B_HEADS):
        k_aug = jnp.concatenate([bk[:, hh * LANES:(hh + 1) * LANES],
                                 jnp.where(lane < 2 * _AUG_SPLITS, ktab_ref[hh:hh + 1, :], pos_cols)], axis=1)
        bk_ref[hh] = k_aug.T.astype(BF16)
        bv_ref[hh, :, 0:LANES] = bv[:, hh * LANES:(hh + 1) * LANES]
        bv_ref[hh, :, LANES:2 * LANES] = ones
    slab = 0
    for part, (c0, scale) in enumerate(((_W_CQ, SCALE), (_W_CK, 1.0), (_W_CV, 1.0))):
        pc = proj(c0, c0 + C_QKV) * scale
        for g, (_, d) in enumerate(C_PAIRS):
            dst = c_refs[3 * part + g]
            val = pc[:, g * HEADS_W:(g + 1) * HEADS_W]
            if d == 1:
                dst[0] = val.astype(BF16)
            else:
                for c in range(HEADS_W // LANES):
                    buf = slabs[slab]
                    slab += 1
                    lanes = slice(c * LANES, (c + 1) * LANES)
                    buf[...] = val[:, lanes]
                    for r in range(d):
                        dst[r, :, lanes] = buf[pl.ds(r, tm // d, stride=d), :].astype(BF16)


def _in_proj(x, g, w, B, S):
    T = x.shape[0]
    tm = min(512, S)
    nt = S // tm
    row = lambda i: (i, 0)
    const = lambda i: (0, 0)
    outs = [
        (jax.ShapeDtypeStruct((T, 256), BF16), pl.BlockSpec((tm, 256), row)),
        (jax.ShapeDtypeStruct((T, 256), BF16), pl.BlockSpec((tm, 256), row)),
        (jax.ShapeDtypeStruct((T, 256), BF16), pl.BlockSpec((tm, 256), row)),
        (jax.ShapeDtypeStruct((T, B_QK), BF16), pl.BlockSpec((tm, B_QK), row)),
        (jax.ShapeDtypeStruct((B_HEADS, B, 256, S), BF16),
         pl.BlockSpec((B_HEADS, None, 256, tm), lambda i: (0, i // nt, 0, i % nt))),
        (jax.ShapeDtypeStruct((B_HEADS, T, 256), BF16), pl.BlockSpec((B_HEADS, tm, 256), lambda i: (0, i, 0))),
    ]
    for _ in range(3):
        for _, d in C_PAIRS:
            outs.append((jax.ShapeDtypeStruct((B, d, S // d, HEADS_W), BF16),
                         pl.BlockSpec((None, d, tm // d, HEADS_W), lambda i: (i // nt, 0, i % nt, 0))))
    n_slabs = 3 * sum(1 for _, d in C_PAIRS if d > 1)
    return pl.pallas_call(
        functools.partial(_inproj_kernel, seq_len=S, tm=tm),
        grid=(T // tm,),
        in_specs=[pl.BlockSpec((tm, D_MODEL), row),
                  pl.BlockSpec((1, D_MODEL), const),
                  pl.BlockSpec((D_MODEL, _W_COLS), const),
                  pl.BlockSpec((B_HEADS, LANES), const)],
        out_specs=[o[1] for o in outs],
        out_shape=[o[0] for o in outs],
        scratch_shapes=[pltpu.VMEM((tm, LANES), F32)] * (n_slabs * (HEADS_W // LANES)),
        compiler_params=pltpu.CompilerParams(dimension_semantics=("parallel",),
                                             vmem_limit_bytes=VMEM_LIMIT_BYTES),
    )(x, g.reshape(1, D_MODEL), w, _diff_alibi_tables()[0])


def _prep_w_in(w):
    g = A_HEADS // A_KV_HEADS
    aq = w[:, :A_Q]
    ak = jnp.repeat(w[:, A_Q:A_Q + A_KV].reshape(D_MODEL, A_KV_HEADS, 1, HEAD_DIM), g, axis=2).reshape(D_MODEL, A_Q)
    av = jnp.repeat(w[:, A_Q + A_KV:A_Q + 2 * A_KV].reshape(D_MODEL, A_KV_HEADS, 1, HEAD_DIM), g, axis=2).reshape(D_MODEL, A_Q)
    return jnp.concatenate([aq, ak, av, w[:, A_Q + 2 * A_KV:]], axis=1).astype(BF16)


_BAND_SUBBLOCKS = 16


def _banded_kernel(*refs, rr, tq, koff, wband, halo, has_sink, emit_lse):
    q_ref, kp_ref, kc_ref, kn_ref, vp_ref, vc_ref, vn_ref, bias_ref, hmask_ref = refs[:9]
    rest = refs[9:]
    if has_sink:
        sink_ref, rest = rest[0], rest[1:]
    o_ref = rest[0]
    lse_ref = rest[1] if emit_lse else None

    first = pl.program_id(2) == 0
    last = pl.program_id(2) == pl.num_programs(2) - 1
    col = lax.broadcasted_iota(jnp.int32, (1, wband), 1)
    lane_head = lax.broadcasted_iota(jnp.int32, (BLOCK, HEADS_W), 1) // HEAD_DIM
    if has_sink:
        sink_col = lax.broadcasted_iota(jnp.int32, (1, LANES), 1) == 0

        def in_col0(x, new):
            return jnp.concatenate([jnp.where(sink_col, new, x[:, :LANES]), x[:, LANES:]], axis=1)
    for r, i in itertools.product(range(rr), range(tq // BLOCK)):
        lo = BLOCK * i - koff
        hi = lo + wband

        def band(p_ref, c_ref, n_ref):
            parts = []
            if lo < 0:
                parts.append(p_ref[r, halo + lo:halo, :])
            parts.append(c_ref[r, max(lo, 0):min(hi, tq), :])
            if hi > tq:
                parts.append(n_ref[r, 0:hi - tq, :])
            return parts[0] if len(parts) == 1 else jnp.concatenate(parts, axis=0)

        kb = band(kp_ref, kc_ref, kn_ref)
        vb = band(vp_ref, vc_ref, vn_ref)
        q = q_ref[r, BLOCK * i:BLOCK * (i + 1), :]
        qs = jnp.concatenate([q] * 4, axis=0) * hmask_ref[...]
        s = _dot_nt(qs, kb) + bias_ref[...]
        if lo < 0:
            s = s + jnp.where(first & (col < -lo), NEG, 0.0)
        if hi > tq:
            s = s + jnp.where(last & (col >= wband - (hi - tq)), NEG, 0.0)
        if has_sink:
            s = in_col0(s, sink_ref[...])
        m = jnp.max(s, axis=-1, keepdims=True)
        p = jnp.exp(s - m)
        l = jnp.sum(p, axis=-1, keepdims=True)
        if has_sink:
            p = in_col0(p, 0.0)
        pv = _dot(p.astype(BF16), vb) / l
        o = jnp.zeros((BLOCK, HEADS_W), F32)
        lse_o = jnp.zeros((BLOCK, HEADS_W), F32)
        lse = m + jnp.log(l) if emit_lse else None
        for h in range(4):
            hr = slice(BLOCK * h, BLOCK * (h + 1))
            o = jnp.where(lane_head == h, pv[hr], o)
            if emit_lse:
                lse_o = jnp.where(lane_head == h, lse[hr], lse_o)
        o_ref[r, BLOCK * i:BLOCK * (i + 1), :] = o.astype(BF16)
        if emit_lse:
            lse_ref[r, BLOCK * i:BLOCK * (i + 1), :] = lse_o


def _band_tables(window, koff, wband, slopes_eff):
    il = np.arange(BLOCK)[:, None]
    c = np.arange(wband)[None, :]
    dist = np.abs(c - koff - il).astype(np.float64)
    bias = np.concatenate([np.where(dist <= window, -s * dist, NEG) for s in slopes_eff], axis=0)
    hmask = np.concatenate([np.broadcast_to((np.arange(HEADS_W) // HEAD_DIM == h)[None, :], (BLOCK, HEADS_W))
                            for h in range(4)], axis=0)
    return jnp.asarray(bias, dtype=F32), jnp.asarray(hmask, dtype=BF16)


def _banded_attention(q, k, v, *, window, slopes_eff, sink=None, emit_lse=False):
    B, d, Sd, _ = q.shape
    tq = min(_BAND_SUBBLOCKS * BLOCK, Sd)
    rr = min(d, max(1, _BAND_SUBBLOCKS * BLOCK // tq))
    koff, wband, halo = (BLOCK // 2, 2 * BLOCK, BLOCK) if window <= BLOCK // 2 else (3 * BLOCK // 2, 4 * BLOCK, 2 * BLOCK)
    halo = min(halo, tq)
    assert koff <= halo and wband - koff - BLOCK <= halo and window <= koff and window <= wband - koff - BLOCK
    nh, nhb = tq // halo, Sd // halo
    bias, hmask = _band_tables(window, koff, wband, slopes_eff)
    cur = pl.BlockSpec((None, rr, tq, HEADS_W), lambda b, r, i: (b, r, i, 0))
    prev = pl.BlockSpec((None, rr, halo, HEADS_W), lambda b, r, i: (b, r, jnp.maximum(i * nh - 1, 0), 0))
    nxt = pl.BlockSpec((None, rr, halo, HEADS_W), lambda b, r, i: (b, r, jnp.minimum((i + 1) * nh, nhb - 1), 0))
    const = lambda b, r, i: (0, 0)
    in_specs = [cur, prev, cur, nxt, prev, cur, nxt,
                pl.BlockSpec((4 * BLOCK, wband), const), pl.BlockSpec((4 * BLOCK, HEADS_W), const)]
    args = [q, k, k, k, v, v, v, bias, hmask]
    if sink is not None:
        assert koff > window
        in_specs.append(pl.BlockSpec((4 * BLOCK, LANES), const))
        args.append(jnp.broadcast_to(jnp.repeat(sink.astype(F32), BLOCK)[:, None], (4 * BLOCK, LANES)))
    out_shape = [jax.ShapeDtypeStruct((B, d, Sd, HEADS_W), BF16)]
    out_specs = [cur]
    if emit_lse:
        out_shape.append(jax.ShapeDtypeStruct((B, d, Sd, HEADS_W), F32))
        out_specs.append(cur)
    return pl.pallas_call(
        functools.partial(_banded_kernel, rr=rr, tq=tq, koff=koff, wband=wband, halo=halo, has_sink=sink is not None,
                          emit_lse=emit_lse),
        grid=(B, d // rr, Sd // tq),
        in_specs=in_specs,
        out_specs=out_specs,
        out_shape=out_shape,
        compiler_params=pltpu.CompilerParams(dimension_semantics=("parallel", "parallel", "parallel"),
                                             vmem_limit_bytes=VMEM_LIMIT_BYTES),
    )(*args)


_SOFTMAX_ROWS = 128
_PIPE_SLOTS = 4


def _diff_kernel(coef_ref, qtab_ref, lam_ref, subln_ref, q_ref, k_ref, v_ref, o_ref,
                 qs_ref, m_ref, acc_ref, *bufs, tq, nk, lam_init):
    qi = pl.program_id(2)
    s_refs, p_refs, a_refs = bufs[0:4], bufs[4:8], bufs[8:12]

    q = q_ref[...].astype(F32)
    lane = lax.broadcasted_iota(jnp.int32, (tq, LANES), 1)
    pos = qi * tq + lax.broadcasted_iota(jnp.int32, (tq, LANES), 0)
    base = jnp.where(lane < _AUG_SPLITS, -((pos >> _BLOCK_LOG2) << _BLOCK_LOG2).astype(F32),
                     jnp.where(lane < 2 * _AUG_SPLITS, -(pos & (BLOCK - 1)).astype(F32), qtab_ref[...]))
    q1 = jnp.where(lane < HEAD_DIM, q, 0.0).astype(BF16)
    q2 = jnp.where(lane >= HEAD_DIM, q, 0.0).astype(BF16)
    for var, sign in enumerate((1.0, -1.0, 0.0)):
        aug = (sign * base).astype(BF16)
        qs_ref[var, 0:tq, 0:LANES] = q1
        qs_ref[var, 0:tq, LANES:2 * LANES] = aug
        qs_ref[var, tq:2 * tq, 0:LANES] = q2
        qs_ref[var, tq:2 * tq, LANES:2 * LANES] = aug
    m_ref[...] = jnp.full(m_ref.shape, NEG, F32)
    acc_ref[...] = jnp.zeros(acc_ref.shape, F32)

    def key_tile(t):
        return jnp.where(t == 0, qi, jnp.where(t <= qi, t - 1, t))

    def rows_of(j):
        return pl.ds(pl.multiple_of(j * tq, tq), tq)

    def stage1_diag(slot):
        s = _dot(qs_ref[2], k_ref[:, rows_of(qi)])
        r = lax.broadcasted_iota(jnp.int32, s.shape, 0) & (tq - 1)
        c = lax.broadcasted_iota(jnp.int32, s.shape, 1)
        s_refs[slot][...] = s - coef_ref[:, 0:1] * jnp.abs(r - c).astype(F32)

    def stage1(t, slot):
        j = key_tile(t)
        s_refs[slot][...] = _dot(qs_ref[jnp.where(j < qi, 0, 1)], k_ref[:, rows_of(j)])

    def stage2(slot):
        s_a, s_b = s_refs[slot], s_refs[slot + 1]
        for r0 in range(0, 2 * tq, _SOFTMAX_ROWS):
            rows = slice(r0, r0 + _SOFTMAX_ROWS)
            m_prev = m_ref[rows, :]
            row_max = jnp.max(jnp.maximum(s_a[rows, :], s_b[rows, :]), axis=-1, keepdims=True)
            m_new = jnp.maximum(m_prev, row_max)
            a_refs[slot][rows, :] = jnp.exp2(m_prev - m_new)
            m_ref[rows, :] = m_new
        for r0 in range(0, 2 * tq, _SOFTMAX_ROWS):
            rows = slice(r0, r0 + _SOFTMAX_ROWS)
            m_new = jnp.concatenate([m_ref[rows, :]] * (tq // LANES), axis=1)
            p_refs[slot][rows, :] = jnp.exp2(s_a[rows, :] - m_new).astype(BF16)
            p_refs[slot + 1][rows, :] = jnp.exp2(s_b[rows, :] - m_new).astype(BF16)

    def stage3(j_a, j_b, slot):
        pv = _dot(p_refs[slot][...], v_ref[rows_of(j_a), :]) + _dot(p_refs[slot + 1][...], v_ref[rows_of(j_b), :])
        a = a_refs[slot][...]
        acc_ref[...] = acc_ref[...] * jnp.concatenate([a, a], axis=1) + pv

    def step(u, first, has1=True, has2=True, has3=True):
        other = 2 - first
        if has2:
            stage2(other)
        if has1:
            stage1(2 * u, first)
            stage1(2 * u + 1, first + 1)
        if has3:
            stage3(key_tile(2 * u - 4), key_tile(2 * u - 3), first)

    stage1_diag(0)
    stage1(1, 1)
    step(1, 2, has3=False)

    for u in range(2, nk // 2):
        step(u, 2 * (u % 2))
    step(nk // 2, 0, has1=False)
    step(nk // 2 + 1, 2, has1=False, has2=False)

    acc = acc_ref[...]
    o1 = acc[:tq, :LANES] / acc[:tq, LANES:]
    o2 = acc[tq:, :LANES] / acc[tq:, LANES:]
    lv = lam_ref[...]
    lam = (jnp.exp(jnp.sum(lv[0:1] * lv[1:2], axis=-1, keepdims=True))
           - jnp.exp(jnp.sum(lv[2:3] * lv[3:4], axis=-1, keepdims=True)) + lam_init)
    o = _rms(o1 - lam * o2, subln_ref[...], SUBLN_EPS) * (1.0 - lam_init)
    o_ref[...] = o.astype(BF16)


def _diff_attention(q, k, v, lam_vecs, subln, *, B, S, lam_init):
    tq = min(512, S // 4)
    nq = S // tq
    assert nq % 4 == 0
    _, qtab, coef = _diff_alibi_tables()
    qv = q.reshape(B, S, B_QK)
    vv = v.reshape(B_HEADS, B, S, 2 * LANES)
    out = pl.pallas_call(
        functools.partial(_diff_kernel, tq=tq, nk=nq, lam_init=lam_init),
        grid=(B, B_HEADS, nq),
        in_specs=[pl.BlockSpec((None, 1, LANES), lambda b, h, i: (h, 0, 0)),
                  pl.BlockSpec((None, 1, LANES), lambda b, h, i: (h, 0, 0)),
                  pl.BlockSpec((4, HEAD_DIM), lambda b, h, i: (0, 0)),
                  pl.BlockSpec((1, B_VDIM), lambda b, h, i: (0, 0)),
                  pl.BlockSpec((None, tq, LANES), lambda b, h, i: (b, i, h)),
                  pl.BlockSpec((None, None, 2 * LANES, S), lambda b, h, i: (h, b, 0, 0)),
                  pl.BlockSpec((None, None, S, 2 * LANES), lambda b, h, i: (h, b, 0, 0))],
        out_specs=pl.BlockSpec((None, tq, LANES), lambda b, h, i: (b, i, h)),
        out_shape=jax.ShapeDtypeStruct((B, S, B_V), BF16),
        scratch_shapes=[pltpu.VMEM((3, 2 * tq, 2 * LANES), BF16),
                        pltpu.VMEM((2 * tq, LANES), F32),
                        pltpu.VMEM((2 * tq, 2 * LANES), F32),
                        *[pltpu.VMEM((2 * tq, tq), F32)] * _PIPE_SLOTS,
                        *[pltpu.VMEM((2 * tq, tq), BF16)] * _PIPE_SLOTS,
                        *[pltpu.VMEM((2 * tq, LANES), F32)] * _PIPE_SLOTS],
        compiler_params=pltpu.CompilerParams(
            dimension_semantics=("parallel", "parallel", "arbitrary"),
            vmem_limit_bytes=VMEM_LIMIT_BYTES),
    )(coef, qtab, lam_vecs.astype(F32), subln.reshape(1, B_VDIM).astype(F32), qv, k, vv)
    return out.reshape(B * S, B_V)


def _mixout_kernel(x_ref, oa_ref, ob_ref, g0_ref, g1_ref, g2_ref, l0_ref, l1_ref, l2_ref, w_ref, ln_ref,
                   xo_ref, h_ref, *slabs, tm):
    slabs = list(slabs)

    def tokens(ref, d):
        if d == 1:
            return ref[0].astype(F32)
        halves = []
        for c in range(HEADS_W // LANES):
            buf = slabs.pop()
            for r in range(d):
                buf[pl.ds(r, tm // d, stride=d), :] = ref[r, :, c * LANES:(c + 1) * LANES].astype(F32)
            halves.append(buf[...])
        return jnp.concatenate(halves, axis=1)

    dil = [d for _, d in C_PAIRS]
    l0, l1, l2 = tokens(l0_ref, dil[0]), tokens(l1_ref, dil[1]), tokens(l2_ref, dil[2])
    g0, g1, g2 = tokens(g0_ref, dil[0]), tokens(g1_ref, dil[1]), tokens(g2_ref, dil[2])
    mx = jnp.maximum(jnp.maximum(l0, l1), l2)
    e0, e1, e2 = jnp.exp(l0 - mx), jnp.exp(l1 - mx), jnp.exp(l2 - mx)
    oc = (e0 * g0 + e1 * g1 + e2 * g2) / (e0 + e1 + e2)
    mix = (_dot(oa_ref[...], w_ref[0:A_Q, :])
           + _dot(ob_ref[...], w_ref[A_Q:A_Q + B_V, :])
           + _dot(oc.astype(BF16), w_ref[A_Q + B_V:, :]))
    x = x_ref[...] + mix
    xo_ref[...] = x
    h_ref[...] = _rms(x, ln_ref[...], EPS).astype(BF16)


def _mix_out(x, oa, ob, og, lse, w_out, ln2, B, S):
    T = x.shape[0]
    tm = min(512, S)
    nt = S // tm
    row = lambda i: (i, 0)
    const = lambda i: (0, 0)
    grp = [pl.BlockSpec((None, d, tm // d, HEADS_W), lambda i: (i // nt, 0, i % nt, 0)) for _, d in C_PAIRS]
    n_slabs = 2 * sum(1 for _, d in C_PAIRS if d > 1)
    return pl.pallas_call(
        functools.partial(_mixout_kernel, tm=tm),
        grid=(T // tm,),
        in_specs=[pl.BlockSpec((tm, D_MODEL), row),
                  pl.BlockSpec((tm, A_Q), row),
                  pl.BlockSpec((tm, B_V), row)]
                 + grp + grp
                 + [pl.BlockSpec((D_MODEL, D_MODEL), const),
                    pl.BlockSpec((1, D_MODEL), const)],
        out_specs=[pl.BlockSpec((tm, D_MODEL), row), pl.BlockSpec((tm, D_MODEL), row)],
        out_shape=[jax.ShapeDtypeStruct((T, D_MODEL), F32), jax.ShapeDtypeStruct((T, D_MODEL), BF16)],
        scratch_shapes=[pltpu.VMEM((tm, LANES), F32)] * (n_slabs * (HEADS_W // LANES)),
        compiler_params=pltpu.CompilerParams(dimension_semantics=("parallel",),
                                             vmem_limit_bytes=VMEM_LIMIT_BYTES),
    )(x, oa, ob, *og, *lse, w_out, ln2.reshape(1, D_MODEL))


_UP_CHUNK = 256
_UP_HALO = 16


def _upconv_kernel(hp_ref, hc_ref, hn_ref, w_ref, cw_ref, cb_ref, o_ref, hh_ref, *, tm, seq_len):
    pos0 = (pl.program_id(0) * tm) % seq_len
    keep_prev = jnp.where(pos0 == 0, 0.0, 1.0)
    keep_next = jnp.where(pos0 + tm == seq_len, 0.0, 1.0)
    ext = tm + 2 * _UP_HALO
    hh_ref[0:_UP_HALO, :] = (hp_ref[...].astype(F32) * keep_prev).astype(BF16)
    hh_ref[_UP_HALO:_UP_HALO + tm, :] = hc_ref[...]
    hh_ref[_UP_HALO + tm:ext, :] = (hn_ref[...].astype(F32) * keep_next).astype(BF16)
    hh = hh_ref[...]

    def conv(c0):
        cols = slice(c0, c0 + _UP_CHUNK)
        u = _dot(hh, w_ref[:, cols])
        return (pltpu.roll(u, 1, axis=0) * cw_ref[0:1, cols] + u * cw_ref[1:2, cols]
                + pltpu.roll(u, ext - 1, axis=0) * cw_ref[2:3, cols] + cb_ref[:, cols])

    for c0 in range(0, 2 * D_FF, _UP_CHUNK):
        o_ref[:, c0:c0 + _UP_CHUNK] = conv(c0)[_UP_HALO:_UP_HALO + tm, :].astype(BF16)


def _up_conv(h, w_up, conv_w, conv_b, *, seq_len):
    T = h.shape[0]
    tm = min(512, T)
    nh = tm // _UP_HALO
    n_halo = T // _UP_HALO
    row = lambda i: (i, 0)
    const = lambda i: (0, 0)
    return pl.pallas_call(
        functools.partial(_upconv_kernel, tm=tm, seq_len=seq_len),
        grid=(T // tm,),
        in_specs=[pl.BlockSpec((_UP_HALO, D_MODEL), lambda i: (jnp.maximum(i * nh - 1, 0), 0)),
                  pl.BlockSpec((tm, D_MODEL), row),
                  pl.BlockSpec((_UP_HALO, D_MODEL), lambda i: (jnp.minimum((i + 1) * nh, n_halo - 1), 0)),
                  pl.BlockSpec((D_MODEL, 2 * D_FF), const),
                  pl.BlockSpec((3, 2 * D_FF), const),
                  pl.BlockSpec((1, 2 * D_FF), const)],
        out_specs=pl.BlockSpec((tm, 2 * D_FF), row),
        out_shape=jax.ShapeDtypeStruct((T, 2 * D_FF), BF16),
        scratch_shapes=[pltpu.VMEM((tm + 2 * _UP_HALO, D_MODEL), BF16)],
        compiler_params=pltpu.CompilerParams(dimension_semantics=("parallel",),
                                             vmem_limit_bytes=VMEM_LIMIT_BYTES),
    )(h, h, h, w_up, conv_w, conv_b.reshape(1, 2 * D_FF))


def _down_kernel(c_ref, w_ref, x_ref, lnf_ref, o_ref, act_ref, *, final_norm):
    for c0 in range(0, D_FF, _UP_CHUNK):
        half = 0.5 * c_ref[:, c0:c0 + _UP_CHUNK].astype(F32)
        val = c_ref[:, D_FF + c0:D_FF + c0 + _UP_CHUNK].astype(F32)
        act_ref[:, c0:c0 + _UP_CHUNK] = ((half + half * jnp.tanh(half)) * val).astype(BF16)
    x = x_ref[...] + _dot(act_ref[...], w_ref[...])
    if final_norm:
        x = _rms(x, lnf_ref[...], EPS)
    o_ref[...] = x


def _down(c, w_down, x, ln_f, *, final_norm):
    T = x.shape[0]
    tm = min(512, T)
    row = lambda i: (i, 0)
    const = lambda i: (0, 0)
    return pl.pallas_call(
        functools.partial(_down_kernel, final_norm=final_norm),
        grid=(T // tm,),
        in_specs=[pl.BlockSpec((tm, 2 * D_FF), row),
                  pl.BlockSpec((D_FF, D_MODEL), const),
                  pl.BlockSpec((tm, D_MODEL), row),
                  pl.BlockSpec((1, D_MODEL), const)],
        out_specs=pl.BlockSpec((tm, D_MODEL), row),
        out_shape=jax.ShapeDtypeStruct((T, D_MODEL), F32),
        scratch_shapes=[pltpu.VMEM((tm, D_FF), BF16)],
        compiler_params=pltpu.CompilerParams(dimension_semantics=("parallel",),
                                             vmem_limit_bytes=VMEM_LIMIT_BYTES),
    )(c, w_down, x, ln_f.reshape(1, D_MODEL))


def _trunk(x, p):
    B, S = x.shape[0], x.shape[1]
    xf = x.reshape(B * S, D_MODEL)
    a_slopes = _alibi_slopes(A_HEADS)
    c_slopes = _alibi_slopes(C_GROUPS * C_HEADS)
    for l in range(DEPTH):
        aq, ak, av, bq, bk, bv, *c = _in_proj(xf, p["ln1"][l], p["w_in"][l], B, S)
        as_seq = lambda t: t.reshape(B, 1, S, HEADS_W)
        (oa,) = _banded_attention(as_seq(aq), as_seq(ak), as_seq(av), window=A_WINDOW, slopes_eff=a_slopes,
                                  sink=p["a_sink"][l])
        lam_init = 0.8 - 0.6 * math.exp(-0.3 * l)
        ob = _diff_attention(bq, bk, bv, p["lam"][l], p["subln"][l], B=B, S=S, lam_init=lam_init)
        og, lse = [], []
        for g, (w, d) in enumerate(C_PAIRS):
            o_g, lse_g = _banded_attention(
                c[g], c[3 + g], c[6 + g], window=w // (2 * d),
                slopes_eff=[s * d for s in c_slopes[g * C_HEADS:(g + 1) * C_HEADS]], emit_lse=True)
            og.append(o_g)
            lse.append(lse_g)
        xf, h2 = _mix_out(xf, oa.reshape(B * S, HEADS_W), ob, og, lse, p["w_out"][l], p["ln2"][l], B, S)
        c_mlp = _up_conv(h2, p["w_up"][l], p["conv_w"][l], p["conv_b"][l], seq_len=S)
        xf = _down(c_mlp, p["w_down"][l], xf, p["ln_f"], final_norm=(l == DEPTH - 1))
    return xf.reshape(B, S, D_MODEL)


def kernel(x_prompt, x_sample, ln1, w_in, a_sink, lam_q1, lam_k1, lam_q2, lam_k2, subln, w_out, ln2, w_up,
           conv_w, conv_b, w_down, ln_f):
    p = {
        "ln1": ln1,
        "w_in": [_prep_w_in(w_in[l]) for l in range(DEPTH)],
        "a_sink": a_sink,
        "lam": jnp.stack([lam_q1, lam_k1, lam_q2, lam_k2], axis=1),
        "subln": subln,
        "w_out": w_out.astype(BF16),
        "ln2": ln2,
        "w_up": w_up.astype(BF16),
        "conv_w": conv_w,
        "conv_b": conv_b,
        "w_down": w_down.astype(BF16),
        "ln_f": ln_f,
    }
    return (_trunk(x_prompt, p), _trunk(x_sample, p))
```

```python
import functools
import itertools
import math

import numpy as np
import jax
import jax.numpy as jnp
from jax import lax
from jax.experimental import pallas as pl
from jax.experimental.pallas import tpu as pltpu

F32 = jnp.float32
BF16 = jnp.bfloat16

D_MODEL = 1024
DEPTH = 2
HEAD_DIM = 64
BLOCK = 128
EPS = 1e-6
SUBLN_EPS = 1e-5
NEG = -1e30
SCALE = HEAD_DIM ** -0.5
A_HEADS = 4
A_KV_HEADS = 2
A_WINDOW = 128
B_HEADS = 4
B_VDIM = 2 * HEAD_DIM
C_PAIRS = ((128, 1), (512, 4), (2048, 16))
C_GROUPS = 3
C_HEADS = 4
D_FF = 2816
A_Q = A_HEADS * HEAD_DIM
A_KV = A_KV_HEADS * HEAD_DIM
B_QK = B_HEADS * 2 * HEAD_DIM
B_V = B_HEADS * B_VDIM
C_QKV = C_GROUPS * C_HEADS * HEAD_DIM
HEADS_W = 4 * HEAD_DIM

VMEM_LIMIT_BYTES = 56 * 1024 * 1024
LANES = 128
SUBLANES = 8


def _alibi_slopes(n):
    return [2.0 ** (-8.0 * k / n) for k in range(1, n + 1)]


LOG2E = math.log2(math.e)
_AUG_SPLITS = 3
_BLOCK_LOG2 = BLOCK.bit_length() - 1


def _bf16_round(x):
    b = np.asarray(x, np.float32).view(np.uint32)
    return ((b + (((b >> 16) & 1) + 0x7FFF)) & 0xFFFF0000).astype(np.uint32).view(np.float32)


def _diff_alibi_tables():
    c = np.asarray(_alibi_slopes(B_HEADS), np.float64) * LOG2E
    pieces, rem = [], c.copy()
    for _ in range(_AUG_SPLITS):
        pk = _bf16_round(rem).astype(np.float64)
        pieces.append(pk)
        rem = rem - pk
    pieces = np.stack(pieces, axis=1)
    ktab = np.zeros((B_HEADS, LANES), np.float32)
    qtab = np.zeros((B_HEADS, 1, LANES), np.float32)
    n = _AUG_SPLITS
    ktab[:, 0:n] = pieces
    ktab[:, n:2 * n] = pieces
    qtab[:, 0, 2 * n:3 * n] = pieces
    qtab[:, 0, 3 * n:4 * n] = pieces
    cfull = np.broadcast_to(c.astype(np.float32)[:, None, None], (B_HEADS, 1, LANES)).copy()
    return jnp.asarray(ktab), jnp.asarray(qtab), jnp.asarray(cfull)


def _dot(a, b):
    return jnp.dot(a, b, preferred_element_type=F32)


def _dot_nt(a, b):
    return lax.dot_general(a, b, (((1,), (1,)), ((), ())), preferred_element_type=F32)


def _rms(x, g, eps):
    return x * lax.rsqrt(jnp.mean(x * x, axis=-1, keepdims=True) + eps) * g


_W_AQ, _W_AK, _W_AV = 0, 256, 512
_W_BQ, _W_BK, _W_BV = 768, 1280, 1792
_W_CQ, _W_CK, _W_CV = 2304, 3072, 3840
_W_COLS = 4608


def _inproj_kernel(x_ref, g_ref, w_ref, ktab_ref, aq_ref, ak_ref, av_ref, bq_ref, bk_ref, bv_ref, *rest, seq_len, tm):
    c_refs, slabs = rest[:9], rest[9:]
    h = _rms(x_ref[...], g_ref[...], EPS).astype(BF16)

    def proj(c0, c1):
        return _dot(h, w_ref[:, c0:c1])

    aq_ref[...] = (proj(_W_AQ, _W_AK) * SCALE).astype(BF16)
    ak_ref[...] = proj(_W_AK, _W_AV).astype(BF16)
    av_ref[...] = proj(_W_AV, _W_BQ).astype(BF16)
    bq_ref[...] = (proj(_W_BQ, _W_BK) * (SCALE * LOG2E)).astype(BF16)
    pos = (pl.program_id(0) * tm + lax.broadcasted_iota(jnp.int32, (tm, LANES), 0)) % seq_len
    lane = lax.broadcasted_iota(jnp.int32, (tm, LANES), 1)
    pos_cols = jnp.where(lane < 3 * _AUG_SPLITS, ((pos >> _BLOCK_LOG2) << _BLOCK_LOG2).astype(F32),
                         jnp.where(lane < 4 * _AUG_SPLITS, (pos & (BLOCK - 1)).astype(F32), 0.0))
    ones = jnp.ones((tm, LANES), BF16)
    bk = proj(_W_BK, _W_BV)
    bv = proj(_W_BV, _W_CQ).astype(BF16)
    for hh in range(B_HEADS):
        k_aug = jnp.concatenate([bk[:, hh * LANES:(hh + 1) * LANES],
                                 jnp.where(lane < 2 * _AUG_SPLITS, ktab_ref[hh:hh + 1, :], pos_cols)], axis=1)
        bk_ref[hh] = k_aug.T.astype(BF16)
        bv_ref[hh, :, 0:LANES] = bv[:, hh * LANES:(hh + 1) * LANES]
        bv_ref[hh, :, LANES:2 * LANES] = ones
    slab = 0
    for part, (c0, scale) in enumerate(((_W_CQ, SCALE), (_W_CK, 1.0), (_W_CV, 1.0))):
        pc = proj(c0, c0 + C_QKV) * scale
        for g, (_, d) in enumerate(C_PAIRS):
            dst = c_refs[3 * part + g]
            val = pc[:, g * HEADS_W:(g + 1) * HEADS_W]
            if d == 1:
                dst[0] = val.astype(BF16)
            else:
                for c in range(HEADS_W // LANES):
                    buf = slabs[slab]
                    slab += 1
                    lanes = slice(c * LANES, (c + 1) * LANES)
                    buf[...] = val[:, lanes]
                    for r in range(d):
                        dst[r, :, lanes] = buf[pl.ds(r, tm // d, stride=d), :].astype(BF16)


def _in_proj(x, g, w, B, S):
    T = x.shape[0]
    tm = min(512, S)
    nt = S // tm
    row = lambda i: (i, 0)
    const = lambda i: (0, 0)
    outs = [
        (jax.ShapeDtypeStruct((T, 256), BF16), pl.BlockSpec((tm, 256), row)),
        (jax.ShapeDtypeStruct((T, 256), BF16), pl.BlockSpec((tm, 256), row)),
        (jax.ShapeDtypeStruct((T, 256), BF16), pl.BlockSpec((tm, 256), row)),
        (jax.ShapeDtypeStruct((T, B_QK), BF16), pl.BlockSpec((tm, B_QK), row)),
        (jax.ShapeDtypeStruct((B_HEADS, B, 256, S), BF16),
         pl.BlockSpec((B_HEADS, None, 256, tm), lambda i: (0, i // nt, 0, i % nt))),
        (jax.ShapeDtypeStruct((B_HEADS, T, 256), BF16), pl.BlockSpec((B_HEADS, tm, 256), lambda i: (0, i, 0))),
    ]
    for _ in range(3):
        for _, d in C_PAIRS:
            outs.append((jax.ShapeDtypeStruct((B, d, S // d, HEADS_W), BF16),
                         pl.BlockSpec((None, d, tm // d, HEADS_W), lambda i: (i // nt, 0, i % nt, 0))))
    n_slabs = 3 * sum(1 for _, d in C_PAIRS if d > 1)
    return pl.pallas_call(
        functools.partial(_inproj_kernel, seq_len=S, tm=tm),
        grid=(T // tm,),
        in_specs=[pl.BlockSpec((tm, D_MODEL), row),
                  pl.BlockSpec((1, D_MODEL), const),
                  pl.BlockSpec((D_MODEL, _W_COLS), const),
                  pl.BlockSpec((B_HEADS, LANES), const)],
        out_specs=[o[1] for o in outs],
        out_shape=[o[0] for o in outs],
        scratch_shapes=[pltpu.VMEM((tm, LANES), F32)] * (n_slabs * (HEADS_W // LANES)),
        compiler_params=pltpu.CompilerParams(dimension_semantics=("parallel",),
                                             vmem_limit_bytes=VMEM_LIMIT_BYTES),
    )(x, g.reshape(1, D_MODEL), w, _diff_alibi_tables()[0])


def _prep_w_in(w):
    g = A_HEADS // A_KV_HEADS
    aq = w[:, :A_Q]
    ak = jnp.repeat(w[:, A_Q:A_Q + A_KV].reshape(D_MODEL, A_KV_HEADS, 1, HEAD_DIM), g, axis=2).reshape(D_MODEL, A_Q)
    av = jnp.repeat(w[:, A_Q + A_KV:A_Q + 2 * A_KV].reshape(D_MODEL, A_KV_HEADS, 1, HEAD_DIM), g, axis=2).reshape(D_MODEL, A_Q)
    return jnp.concatenate([aq, ak, av, w[:, A_Q + 2 * A_KV:]], axis=1).astype(BF16)


_BAND_SUBBLOCKS = 16


def _banded_kernel(*refs, rr, tq, koff, wband, halo, has_sink, emit_lse):
    q_ref, kp_ref, kc_ref, kn_ref, vp_ref, vc_ref, vn_ref, bias_ref, hmask_ref = refs[:9]
    rest = refs[9:]
    if has_sink:
        sink_ref, rest = rest[0], rest[1:]
    o_ref = rest[0]
    lse_ref = rest[1] if emit_lse else None

    first = pl.program_id(2) == 0
    last = pl.program_id(2) == pl.num_programs(2) - 1
    col = lax.broadcasted_iota(jnp.int32, (1, wband), 1)
    lane_head = lax.broadcasted_iota(jnp.int32, (BLOCK, HEADS_W), 1) // HEAD_DIM
    if has_sink:
        sink_col = lax.broadcasted_iota(jnp.int32, (1, LANES), 1) == 0

        def in_col0(x, new):
            return jnp.concatenate([jnp.where(sink_col, new, x[:, :LANES]), x[:, LANES:]], axis=1)
    for r, i in itertools.product(range(rr), range(tq // BLOCK)):
        lo = BLOCK * i - koff
        hi = lo + wband

        def band(p_ref, c_ref, n_ref):
            parts = []
            if lo < 0:
                parts.append(p_ref[r, halo + lo:halo, :])
            parts.append(c_ref[r, max(lo, 0):min(hi, tq), :])
            if hi > tq:
                parts.append(n_ref[r, 0:hi - tq, :])
            return parts[0] if len(parts) == 1 else jnp.concatenate(parts, axis=0)

        kb = band(kp_ref, kc_ref, kn_ref)
        vb = band(vp_ref, vc_ref, vn_ref)
        q = q_ref[r, BLOCK * i:BLOCK * (i + 1), :]
        qs = jnp.concatenate([q] * 4, axis=0) * hmask_ref[...]
        s = _dot_nt(qs, kb) + bias_ref[...]
        if lo < 0:
            s = s + jnp.where(first & (col < -lo), NEG, 0.0)
        if hi > tq:
            s = s + jnp.where(last & (col >= wband - (hi - tq)), NEG, 0.0)
        if has_sink:
            s = in_col0(s, sink_ref[...])
        m = jnp.max(s, axis=-1, keepdims=True)
        p = jnp.exp(s - m)
        l = jnp.sum(p, axis=-1, keepdims=True)
        if has_sink:
            p = in_col0(p, 0.0)
        pv = _dot(p.astype(BF16), vb) / l
        o = jnp.zeros((BLOCK, HEADS_W), F32)
        lse_o = jnp.zeros((BLOCK, HEADS_W), F32)
        lse = m + jnp.log(l) if emit_lse else None
        for h in range(4):
            hr = slice(BLOCK * h, BLOCK * (h + 1))
            o = jnp.where(lane_head == h, pv[hr], o)
            if emit_lse:
                lse_o = jnp.where(lane_head == h, lse[hr], lse_o)
        o_ref[r, BLOCK * i:BLOCK * (i + 1), :] = o.astype(BF16)
        if emit_lse:
            lse_ref[r, BLOCK * i:BLOCK * (i + 1), :] = lse_o


def _band_tables(window, koff, wband, slopes_eff):
    il = np.arange(BLOCK)[:, None]
    c = np.arange(wband)[None, :]
    dist = np.abs(c - koff - il).astype(np.float64)
    bias = np.concatenate([np.where(dist <= window, -s * dist, NEG) for s in slopes_eff], axis=0)
    hmask = np.concatenate([np.broadcast_to((np.arange(HEADS_W) // HEAD_DIM == h)[None, :], (BLOCK, HEADS_W))
                            for h in range(4)], axis=0)
    return jnp.asarray(bias, dtype=F32), jnp.asarray(hmask, dtype=BF16)


def _banded_attention(q, k, v, *, window, slopes_eff, sink=None, emit_lse=False):
    B, d, Sd, _ = q.shape
    tq = min(_BAND_SUBBLOCKS * BLOCK, Sd)
    rr = min(d, max(1, _BAND_SUBBLOCKS * BLOCK // tq))
    koff, wband, halo = (BLOCK // 2, 2 * BLOCK, BLOCK) if window <= BLOCK // 2 else (3 * BLOCK // 2, 4 * BLOCK, 2 * BLOCK)
    halo = min(halo, tq)
    assert koff <= halo and wband - koff - BLOCK <= halo and window <= koff and window <= wband - koff - BLOCK
    nh, nhb = tq // halo, Sd // halo
    bias, hmask = _band_tables(window, koff, wband, slopes_eff)
    cur = pl.BlockSpec((None, rr, tq, HEADS_W), lambda b, r, i: (b, r, i, 0))
    prev = pl.BlockSpec((None, rr, halo, HEADS_W), lambda b, r, i: (b, r, jnp.maximum(i * nh - 1, 0), 0))
    nxt = pl.BlockSpec((None, rr, halo, HEADS_W), lambda b, r, i: (b, r, jnp.minimum((i + 1) * nh, nhb - 1), 0))
    const = lambda b, r, i: (0, 0)
    in_specs = [cur, prev, cur, nxt, prev, cur, nxt,
                pl.BlockSpec((4 * BLOCK, wband), const), pl.BlockSpec((4 * BLOCK, HEADS_W), const)]
    args = [q, k, k, k, v, v, v, bias, hmask]
    if sink is not None:
        assert koff > window
        in_specs.append(pl.BlockSpec((4 * BLOCK, LANES), const))
        args.append(jnp.broadcast_to(jnp.repeat(sink.astype(F32), BLOCK)[:, None], (4 * BLOCK, LANES)))
    out_shape = [jax.ShapeDtypeStruct((B, d, Sd, HEADS_W), BF16)]
    out_specs = [cur]
    if emit_lse:
        out_shape.append(jax.ShapeDtypeStruct((B, d, Sd, HEADS_W), F32))
        out_specs.append(cur)
    return pl.pallas_call(
        functools.partial(_banded_kernel, rr=rr, tq=tq, koff=koff, wband=wband, halo=halo, has_sink=sink is not None,
                          emit_lse=emit_lse),
        grid=(B, d // rr, Sd // tq),
        in_specs=in_specs,
        out_specs=out_specs,
        out_shape=out_shape,
        compiler_params=pltpu.CompilerParams(dimension_semantics=("parallel", "parallel", "parallel"),
                                             vmem_limit_bytes=VMEM_LIMIT_BYTES),
    )(*args)


_SOFTMAX_ROWS = 128
_PIPE_SLOTS = 4
_DIFF_ITEMS = 32


def _diff_kernel(coef_ref, qtab_ref, lam_ref, subln_ref, q_ref, k_ref, v_ref, o_ref,
                 qs_ref, m_ref, acc_ref, *bufs, tq, nk, nqt, lam_init):
    s_refs, p_refs, a_refs = bufs[0:4], bufs[4:8], bufs[8:12]
    lane = lax.broadcasted_iota(jnp.int32, (tq, LANES), 1)
    row = lax.broadcasted_iota(jnp.int32, (tq, LANES), 0)

    def q_tile(w):
        return pl.program_id(2) * nqt + w

    for w in range(nqt):
        q = q_ref[w * tq:(w + 1) * tq, :].astype(F32)
        pos = q_tile(w) * tq + row
        base = jnp.where(lane < _AUG_SPLITS, -((pos >> _BLOCK_LOG2) << _BLOCK_LOG2).astype(F32),
                         jnp.where(lane < 2 * _AUG_SPLITS, -(pos & (BLOCK - 1)).astype(F32), qtab_ref[...]))
        q1 = jnp.where(lane < HEAD_DIM, q, 0.0).astype(BF16)
        q2 = jnp.where(lane >= HEAD_DIM, q, 0.0).astype(BF16)
        for var, sign in enumerate((1.0, -1.0, 0.0)):
            aug = (sign * base).astype(BF16)
            qs_ref[w, var, 0:tq, 0:LANES] = q1
            qs_ref[w, var, 0:tq, LANES:2 * LANES] = aug
            qs_ref[w, var, tq:2 * tq, 0:LANES] = q2
            qs_ref[w, var, tq:2 * tq, LANES:2 * LANES] = aug
    m_ref[...] = jnp.full(m_ref.shape, NEG, F32)
    acc_ref[...] = jnp.zeros(acc_ref.shape, F32)

    def key_tile(n):
        w, t = divmod(n, nk)
        qi = q_tile(w)
        return qi if t == 0 else jnp.where(t <= qi, t - 1, t)

    def rows_of(j):
        return pl.ds(pl.multiple_of(j * tq, tq), tq)

    def stage1(n):
        w, t = divmod(n, nk)
        qi, j = q_tile(w), key_tile(n)
        if t == 0:
            s = _dot(qs_ref[w, 2], k_ref[:, rows_of(j)])
            r = lax.broadcasted_iota(jnp.int32, s.shape, 0) & (tq - 1)
            c = lax.broadcasted_iota(jnp.int32, s.shape, 1)
            s = s - coef_ref[:, 0:1] * jnp.abs(r - c).astype(F32)
        else:
            s = _dot(qs_ref[w, jnp.where(j < qi, 0, 1)], k_ref[:, rows_of(j)])
        s_refs[n % _PIPE_SLOTS][...] = s

    def stage2(n):
        w, slot = n // nk, n % _PIPE_SLOTS
        s_a, s_b = s_refs[slot], s_refs[slot + 1]
        for r0 in range(0, 2 * tq, _SOFTMAX_ROWS):
            rows = slice(r0, r0 + _SOFTMAX_ROWS)
            m_prev = m_ref[w, rows, :]
            row_max = jnp.max(jnp.maximum(s_a[rows, :], s_b[rows, :]), axis=-1, keepdims=True)
            m_new = jnp.maximum(m_prev, row_max)
            a_refs[slot][rows, :] = jnp.exp2(m_prev - m_new)
            m_ref[w, rows, :] = m_new
        for r0 in range(0, 2 * tq, _SOFTMAX_ROWS):
            rows = slice(r0, r0 + _SOFTMAX_ROWS)
            m_new = jnp.concatenate([m_ref[w, rows, :]] * (tq // LANES), axis=1)
            p_refs[slot][rows, :] = jnp.exp2(s_a[rows, :] - m_new).astype(BF16)
            p_refs[slot + 1][rows, :] = jnp.exp2(s_b[rows, :] - m_new).astype(BF16)

    def stage3(n):
        w, slot = n // nk, n % _PIPE_SLOTS
        pv = (_dot(p_refs[slot][...], v_ref[rows_of(key_tile(n)), :])
              + _dot(p_refs[slot + 1][...], v_ref[rows_of(key_tile(n + 1)), :]))
        a = a_refs[slot][...]
        acc_ref[w] = acc_ref[w] * jnp.concatenate([a, a], axis=1) + pv

    def finish(w):
        acc = acc_ref[w]
        o1 = acc[:tq, :LANES] / acc[:tq, LANES:]
        o2 = acc[tq:, :LANES] / acc[tq:, LANES:]
        lv = lam_ref[...]
        lam = (jnp.exp(jnp.sum(lv[0:1] * lv[1:2], axis=-1, keepdims=True))
               - jnp.exp(jnp.sum(lv[2:3] * lv[3:4], axis=-1, keepdims=True)) + lam_init)
        o = _rms(o1 - lam * o2, subln_ref[...], SUBLN_EPS) * (1.0 - lam_init)
        o_ref[w * tq:(w + 1) * tq, :] = o.astype(BF16)

    items = nqt * nk
    for u in range(items // 2 + 2):
        if 0 <= 2 * u - 2 < items:
            stage2(2 * u - 2)
        if 2 * u < items:
            stage1(2 * u)
            stage1(2 * u + 1)
        if 0 <= 2 * u - 4 < items:
            stage3(2 * u - 4)
            if (2 * u - 4) % nk == nk - 2:
                finish((2 * u - 4) // nk)


def _diff_attention(q, k, v, lam_vecs, subln, *, B, S, lam_init):
    tq = min(512, S // 4)
    nk = S // tq
    assert nk % 4 == 0
    nqt = max(1, min(nk, _DIFF_ITEMS // nk))
    _, qtab, coef = _diff_alibi_tables()
    qv = q.reshape(B, S, B_QK)
    vv = v.reshape(B_HEADS, B, S, 2 * LANES)
    out = pl.pallas_call(
        functools.partial(_diff_kernel, tq=tq, nk=nk, nqt=nqt, lam_init=lam_init),
        grid=(B, B_HEADS, nk // nqt),
        in_specs=[pl.BlockSpec((None, 1, LANES), lambda b, h, i: (h, 0, 0)),
                  pl.BlockSpec((None, 1, LANES), lambda b, h, i: (h, 0, 0)),
                  pl.BlockSpec((4, HEAD_DIM), lambda b, h, i: (0, 0)),
                  pl.BlockSpec((1, B_VDIM), lambda b, h, i: (0, 0)),
                  pl.BlockSpec((None, nqt * tq, LANES), lambda b, h, i: (b, i, h)),
                  pl.BlockSpec((None, None, 2 * LANES, S), lambda b, h, i: (h, b, 0, 0)),
                  pl.BlockSpec((None, None, S, 2 * LANES), lambda b, h, i: (h, b, 0, 0))],
        out_specs=pl.BlockSpec((None, nqt * tq, LANES), lambda b, h, i: (b, i, h)),
        out_shape=jax.ShapeDtypeStruct((B, S, B_V), BF16),
        scratch_shapes=[pltpu.VMEM((nqt, 3, 2 * tq, 2 * LANES), BF16),
                        pltpu.VMEM((nqt, 2 * tq, LANES), F32),
                        pltpu.VMEM((nqt, 2 * tq, 2 * LANES), F32),
                        *[pltpu.VMEM((2 * tq, tq), F32)] * _PIPE_SLOTS,
                        *[pltpu.VMEM((2 * tq, tq), BF16)] * _PIPE_SLOTS,
                        *[pltpu.VMEM((2 * tq, LANES), F32)] * _PIPE_SLOTS],
        compiler_params=pltpu.CompilerParams(
            dimension_semantics=("parallel", "parallel", "arbitrary"),
            vmem_limit_bytes=VMEM_LIMIT_BYTES),
    )(coef, qtab, lam_vecs.astype(F32), subln.reshape(1, B_VDIM).astype(F32), qv, k, vv)
    return out.reshape(B * S, B_V)


def _mixout_kernel(x_ref, oa_ref, ob_ref, g0_ref, g1_ref, g2_ref, l0_ref, l1_ref, l2_ref, w_ref, ln_ref,
                   xo_ref, h_ref, *slabs, tm):
    slabs = list(slabs)

    def tokens(ref, d):
        if d == 1:
            return ref[0].astype(F32)
        halves = []
        for c in range(HEADS_W // LANES):
            buf = slabs.pop()
            for r in range(d):
                buf[pl.ds(r, tm // d, stride=d), :] = ref[r, :, c * LANES:(c + 1) * LANES].astype(F32)
            halves.append(buf[...])
        return jnp.concatenate(halves, axis=1)

    dil = [d for _, d in C_PAIRS]
    l0, l1, l2 = tokens(l0_ref, dil[0]), tokens(l1_ref, dil[1]), tokens(l2_ref, dil[2])
    g0, g1, g2 = tokens(g0_ref, dil[0]), tokens(g1_ref, dil[1]), tokens(g2_ref, dil[2])
    mx = jnp.maximum(jnp.maximum(l0, l1), l2)
    e0, e1, e2 = jnp.exp(l0 - mx), jnp.exp(l1 - mx), jnp.exp(l2 - mx)
    oc = (e0 * g0 + e1 * g1 + e2 * g2) / (e0 + e1 + e2)
    mix = (_dot(oa_ref[...], w_ref[0:A_Q, :])
           + _dot(ob_ref[...], w_ref[A_Q:A_Q + B_V, :])
           + _dot(oc.astype(BF16), w_ref[A_Q + B_V:, :]))
    x = x_ref[...] + mix
    xo_ref[...] = x
    h_ref[...] = _rms(x, ln_ref[...], EPS).astype(BF16)


def _mix_out(x, oa, ob, og, lse, w_out, ln2, B, S):
    T = x.shape[0]
    tm = min(512, S)
    nt = S // tm
    row = lambda i: (i, 0)
    const = lambda i: (0, 0)
    grp = [pl.BlockSpec((None, d, tm // d, HEADS_W), lambda i: (i // nt, 0, i % nt, 0)) for _, d in C_PAIRS]
    n_slabs = 2 * sum(1 for _, d in C_PAIRS if d > 1)
    return pl.pallas_call(
        functools.partial(_mixout_kernel, tm=tm),
        grid=(T // tm,),
        in_specs=[pl.BlockSpec((tm, D_MODEL), row),
                  pl.BlockSpec((tm, A_Q), row),
                  pl.BlockSpec((tm, B_V), row)]
                 + grp + grp
                 + [pl.BlockSpec((D_MODEL, D_MODEL), const),
                    pl.BlockSpec((1, D_MODEL), const)],
        out_specs=[pl.BlockSpec((tm, D_MODEL), row), pl.BlockSpec((tm, D_MODEL), row)],
        out_shape=[jax.ShapeDtypeStruct((T, D_MODEL), F32), jax.ShapeDtypeStruct((T, D_MODEL), BF16)],
        scratch_shapes=[pltpu.VMEM((tm, LANES), F32)] * (n_slabs * (HEADS_W // LANES)),
        compiler_params=pltpu.CompilerParams(dimension_semantics=("parallel",),
                                             vmem_limit_bytes=VMEM_LIMIT_BYTES),
    )(x, oa, ob, *og, *lse, w_out, ln2.reshape(1, D_MODEL))


_UP_CHUNK = 256
_UP_HALO = 16


def _upconv_kernel(hp_ref, hc_ref, hn_ref, w_ref, cw_ref, cb_ref, o_ref, hh_ref, *, tm, seq_len):
    pos0 = (pl.program_id(0) * tm) % seq_len
    keep_prev = jnp.where(pos0 == 0, 0.0, 1.0)
    keep_next = jnp.where(pos0 + tm == seq_len, 0.0, 1.0)
    ext = tm + 2 * _UP_HALO
    hh_ref[0:_UP_HALO, :] = (hp_ref[...].astype(F32) * keep_prev).astype(BF16)
    hh_ref[_UP_HALO:_UP_HALO + tm, :] = hc_ref[...]
    hh_ref[_UP_HALO + tm:ext, :] = (hn_ref[...].astype(F32) * keep_next).astype(BF16)
    hh = hh_ref[...]

    def conv(c0):
        cols = slice(c0, c0 + _UP_CHUNK)
        u = _dot(hh, w_ref[:, cols])
        return (pltpu.roll(u, 1, axis=0) * cw_ref[0:1, cols] + u * cw_ref[1:2, cols]
                + pltpu.roll(u, ext - 1, axis=0) * cw_ref[2:3, cols] + cb_ref[:, cols])

    for c0 in range(0, 2 * D_FF, _UP_CHUNK):
        o_ref[:, c0:c0 + _UP_CHUNK] = conv(c0)[_UP_HALO:_UP_HALO + tm, :].astype(BF16)


def _up_conv(h, w_up, conv_w, conv_b, *, seq_len):
    T = h.shape[0]
    tm = min(512, T)
    nh = tm // _UP_HALO
    n_halo = T // _UP_HALO
    row = lambda i: (i, 0)
    const = lambda i: (0, 0)
    return pl.pallas_call(
        functools.partial(_upconv_kernel, tm=tm, seq_len=seq_len),
        grid=(T // tm,),
        in_specs=[pl.BlockSpec((_UP_HALO, D_MODEL), lambda i: (jnp.maximum(i * nh - 1, 0), 0)),
                  pl.BlockSpec((tm, D_MODEL), row),
                  pl.BlockSpec((_UP_HALO, D_MODEL), lambda i: (jnp.minimum((i + 1) * nh, n_halo - 1), 0)),
                  pl.BlockSpec((D_MODEL, 2 * D_FF), const),
                  pl.BlockSpec((3, 2 * D_FF), const),
                  pl.BlockSpec((1, 2 * D_FF), const)],
        out_specs=pl.BlockSpec((tm, 2 * D_FF), row),
        out_shape=jax.ShapeDtypeStruct((T, 2 * D_FF), BF16),
        scratch_shapes=[pltpu.VMEM((tm + 2 * _UP_HALO, D_MODEL), BF16)],
        compiler_params=pltpu.CompilerParams(dimension_semantics=("parallel",),
                                             vmem_limit_bytes=VMEM_LIMIT_BYTES),
    )(h, h, h, w_up, conv_w, conv_b.reshape(1, 2 * D_FF))


def _down_kernel(c_ref, w_ref, x_ref, lnf_ref, o_ref, act_ref, *, final_norm):
    for c0 in range(0, D_FF, _UP_CHUNK):
        half = 0.5 * c_ref[:, c0:c0 + _UP_CHUNK].astype(F32)
        val = c_ref[:, D_FF + c0:D_FF + c0 + _UP_CHUNK].astype(F32)
        act_ref[:, c0:c0 + _UP_CHUNK] = ((half + half * jnp.tanh(half)) * val).astype(BF16)
    x = x_ref[...] + _dot(act_ref[...], w_ref[...])
    if final_norm:
        x = _rms(x, lnf_ref[...], EPS)
    o_ref[...] = x


def _down(c, w_down, x, ln_f, *, final_norm):
    T = x.shape[0]
    tm = min(512, T)
    row = lambda i: (i, 0)
    const = lambda i: (0, 0)
    return pl.pallas_call(
        functools.partial(_down_kernel, final_norm=final_norm),
        grid=(T // tm,),
        in_specs=[pl.BlockSpec((tm, 2 * D_FF), row),
                  pl.BlockSpec((D_FF, D_MODEL), const),
                  pl.BlockSpec((tm, D_MODEL), row),
                  pl.BlockSpec((1, D_MODEL), const)],
        out_specs=pl.BlockSpec((tm, D_MODEL), row),
        out_shape=jax.ShapeDtypeStruct((T, D_MODEL), F32),
        scratch_shapes=[pltpu.VMEM((tm, D_FF), BF16)],
        compiler_params=pltpu.CompilerParams(dimension_semantics=("parallel",),
                                             vmem_limit_bytes=VMEM_LIMIT_BYTES),
    )(c, w_down, x, ln_f.reshape(1, D_MODEL))


def _trunk(x, p):
    B, S = x.shape[0], x.shape[1]
    xf = x.reshape(B * S, D_MODEL)
    a_slopes = _alibi_slopes(A_HEADS)
    c_slopes = _alibi_slopes(C_GROUPS * C_HEADS)
    for l in range(DEPTH):
        aq, ak, av, bq, bk, bv, *c = _in_proj(xf, p["ln1"][l], p["w_in"][l], B, S)
        as_seq = lambda t: t.reshape(B, 1, S, HEADS_W)
        (oa,) = _banded_attention(as_seq(aq), as_seq(ak), as_seq(av), window=A_WINDOW, slopes_eff=a_slopes,
                                  sink=p["a_sink"][l])
        lam_init = 0.8 - 0.6 * math.exp(-0.3 * l)
        ob = _diff_attention(bq, bk, bv, p["lam"][l], p["subln"][l], B=B, S=S, lam_init=lam_init)
        og, lse = [], []
        for g, (w, d) in enumerate(C_PAIRS):
            o_g, lse_g = _banded_attention(
                c[g], c[3 + g], c[6 + g], window=w // (2 * d),
                slopes_eff=[s * d for s in c_slopes[g * C_HEADS:(g + 1) * C_HEADS]], emit_lse=True)
            og.append(o_g)
            lse.append(lse_g)
        xf, h2 = _mix_out(xf, oa.reshape(B * S, HEADS_W), ob, og, lse, p["w_out"][l], p["ln2"][l], B, S)
        c_mlp = _up_conv(h2, p["w_up"][l], p["conv_w"][l], p["conv_b"][l], seq_len=S)
        xf = _down(c_mlp, p["w_down"][l], xf, p["ln_f"], final_norm=(l == DEPTH - 1))
    return xf.reshape(B, S, D_MODEL)


def kernel(x_prompt, x_sample, ln1, w_in, a_sink, lam_q1, lam_k1, lam_q2, lam_k2, subln, w_out, ln2, w_up,
           conv_w, conv_b, w_down, ln_f):
    p = {
        "ln1": ln1,
        "w_in": [_prep_w_in(w_in[l]) for l in range(DEPTH)],
        "a_sink": a_sink,
        "lam": jnp.stack([lam_q1, lam_k1, lam_q2, lam_k2], axis=1),
        "subln": subln,
        "w_out": w_out.astype(BF16),
        "ln2": ln2,
        "w_up": w_up.astype(BF16),
        "conv_w": conv_w,
        "conv_b": conv_b,
        "w_down": w_down.astype(BF16),
        "ln_f": ln_f,
    }
    return (_trunk(x_prompt, p), _trunk(x_sample, p))
```

```python
import functools
import itertools
import math

import numpy as np
import jax
import jax.numpy as jnp
from jax import lax
from jax.experimental import pallas as pl
from jax.experimental.pallas import tpu as pltpu

F32 = jnp.float32
BF16 = jnp.bfloat16

D_MODEL = 1024
DEPTH = 2
HEAD_DIM = 64
BLOCK = 128
EPS = 1e-6
SUBLN_EPS = 1e-5
NEG = -1e30
SCALE = HEAD_DIM ** -0.5
A_HEADS = 4
A_KV_HEADS = 2
A_WINDOW = 128
B_HEADS = 4
B_VDIM = 2 * HEAD_DIM
C_PAIRS = ((128, 1), (512, 4), (2048, 16))
C_GROUPS = 3
C_HEADS = 4
D_FF = 2816
A_Q = A_HEADS * HEAD_DIM
A_KV = A_KV_HEADS * HEAD_DIM
B_QK = B_HEADS * 2 * HEAD_DIM
B_V = B_HEADS * B_VDIM
C_QKV = C_GROUPS * C_HEADS * HEAD_DIM
HEADS_W = 4 * HEAD_DIM

VMEM_LIMIT_BYTES = 56 * 1024 * 1024
LANES = 128
SUBLANES = 8
_ROW_TILE = 1024
_RESIDENT = pl.Buffered(1)


def _alibi_slopes(n):
    return [2.0 ** (-8.0 * k / n) for k in range(1, n + 1)]


LOG2E = math.log2(math.e)
_AUG_SPLITS = 3
_BLOCK_LOG2 = BLOCK.bit_length() - 1


def _bf16_round(x):
    b = np.asarray(x, np.float32).view(np.uint32)
    return ((b + (((b >> 16) & 1) + 0x7FFF)) & 0xFFFF0000).astype(np.uint32).view(np.float32)


def _diff_alibi_tables():
    c = np.asarray(_alibi_slopes(B_HEADS), np.float64) * LOG2E
    pieces, rem = [], c.copy()
    for _ in range(_AUG_SPLITS):
        pk = _bf16_round(rem).astype(np.float64)
        pieces.append(pk)
        rem = rem - pk
    pieces = np.stack(pieces, axis=1)
    ktab = np.zeros((B_HEADS, LANES), np.float32)
    qtab = np.zeros((B_HEADS, 1, LANES), np.float32)
    n = _AUG_SPLITS
    ktab[:, 0:n] = pieces
    ktab[:, n:2 * n] = pieces
    qtab[:, 0, 2 * n:3 * n] = pieces
    qtab[:, 0, 3 * n:4 * n] = pieces
    cfull = np.broadcast_to(c.astype(np.float32)[:, None, None], (B_HEADS, 1, LANES)).copy()
    return jnp.asarray(ktab), jnp.asarray(qtab), jnp.asarray(cfull)


def _dot(a, b):
    return jnp.dot(a, b, preferred_element_type=F32)


def _dot_nt(a, b):
    return lax.dot_general(a, b, (((1,), (1,)), ((), ())), preferred_element_type=F32)


def _rms(x, g, eps):
    return x * lax.rsqrt(jnp.mean(x * x, axis=-1, keepdims=True) + eps) * g


_W_AQ, _W_AK, _W_AV = 0, 256, 512
_W_BQ, _W_BK, _W_BV = 768, 1280, 1792
_W_CQ, _W_CK, _W_CV = 2304, 3072, 3840
_W_COLS = 4608


def _inproj_kernel(x_ref, g_ref, w_ref, ktab_ref, aq_ref, ak_ref, av_ref, bq_ref, bk_ref, bv_ref, *rest, seq_len, tm):
    c_refs, slabs = rest[:9], rest[9:]
    h = _rms(x_ref[...], g_ref[...], EPS).astype(BF16)

    def proj(c0, c1):
        return _dot(h, w_ref[:, c0:c1])

    aq_ref[...] = (proj(_W_AQ, _W_AK) * SCALE).astype(BF16)
    ak_ref[...] = proj(_W_AK, _W_AV).astype(BF16)
    av_ref[...] = proj(_W_AV, _W_BQ).astype(BF16)
    bq_ref[...] = (proj(_W_BQ, _W_BK) * (SCALE * LOG2E)).astype(BF16)
    pos = (pl.program_id(0) * tm + lax.broadcasted_iota(jnp.int32, (tm, LANES), 0)) % seq_len
    lane = lax.broadcasted_iota(jnp.int32, (tm, LANES), 1)
    pos_cols = jnp.where(lane < 3 * _AUG_SPLITS, ((pos >> _BLOCK_LOG2) << _BLOCK_LOG2).astype(F32),
                         jnp.where(lane < 4 * _AUG_SPLITS, (pos & (BLOCK - 1)).astype(F32), 0.0))
    ones = jnp.ones((tm, LANES), BF16)
    bk = proj(_W_BK, _W_BV)
    bv = proj(_W_BV, _W_CQ).astype(BF16)
    for hh in range(B_HEADS):
        k_aug = jnp.concatenate([bk[:, hh * LANES:(hh + 1) * LANES],
                                 jnp.where(lane < 2 * _AUG_SPLITS, ktab_ref[hh:hh + 1, :], pos_cols)], axis=1)
        bk_ref[hh] = k_aug.T.astype(BF16)
        bv_ref[hh, :, 0:LANES] = bv[:, hh * LANES:(hh + 1) * LANES]
        bv_ref[hh, :, LANES:2 * LANES] = ones
    slab = 0
    for part, (c0, scale) in enumerate(((_W_CQ, SCALE), (_W_CK, 1.0), (_W_CV, 1.0))):
        pc = proj(c0, c0 + C_QKV) * scale
        for g, (_, d) in enumerate(C_PAIRS):
            dst = c_refs[3 * part + g]
            val = pc[:, g * HEADS_W:(g + 1) * HEADS_W]
            if d == 1:
                dst[0] = val.astype(BF16)
            else:
                for c in range(HEADS_W // LANES):
                    buf = slabs[slab]
                    slab += 1
                    lanes = slice(c * LANES, (c + 1) * LANES)
                    buf[...] = val[:, lanes]
                    for r in range(d):
                        dst[r, :, lanes] = buf[pl.ds(r, tm // d, stride=d), :].astype(BF16)


def _in_proj(x, g, w, B, S):
    T = x.shape[0]
    tm = min(_ROW_TILE, S)
    nt = S // tm
    row = lambda i: (i, 0)
    const = lambda i: (0, 0)
    outs = [
        (jax.ShapeDtypeStruct((T, 256), BF16), pl.BlockSpec((tm, 256), row)),
        (jax.ShapeDtypeStruct((T, 256), BF16), pl.BlockSpec((tm, 256), row)),
        (jax.ShapeDtypeStruct((T, 256), BF16), pl.BlockSpec((tm, 256), row)),
        (jax.ShapeDtypeStruct((T, B_QK), BF16), pl.BlockSpec((tm, B_QK), row)),
        (jax.ShapeDtypeStruct((B_HEADS, B, 256, S), BF16),
         pl.BlockSpec((B_HEADS, None, 256, tm), lambda i: (0, i // nt, 0, i % nt))),
        (jax.ShapeDtypeStruct((B_HEADS, T, 256), BF16), pl.BlockSpec((B_HEADS, tm, 256), lambda i: (0, i, 0))),
    ]
    for _ in range(3):
        for _, d in C_PAIRS:
            outs.append((jax.ShapeDtypeStruct((B, d, S // d, HEADS_W), BF16),
                         pl.BlockSpec((None, d, tm // d, HEADS_W), lambda i: (i // nt, 0, i % nt, 0))))
    n_slabs = 3 * sum(1 for _, d in C_PAIRS if d > 1)
    return pl.pallas_call(
        functools.partial(_inproj_kernel, seq_len=S, tm=tm),
        grid=(T // tm,),
        in_specs=[pl.BlockSpec((tm, D_MODEL), row),
                  pl.BlockSpec((1, D_MODEL), const),
                  pl.BlockSpec((D_MODEL, _W_COLS), const, pipeline_mode=_RESIDENT),
                  pl.BlockSpec((B_HEADS, LANES), const)],
        out_specs=[o[1] for o in outs],
        out_shape=[o[0] for o in outs],
        scratch_shapes=[pltpu.VMEM((tm, LANES), F32)] * (n_slabs * (HEADS_W // LANES)),
        compiler_params=pltpu.CompilerParams(dimension_semantics=("parallel",),
                                             vmem_limit_bytes=VMEM_LIMIT_BYTES),
    )(x, g.reshape(1, D_MODEL), w, _diff_alibi_tables()[0])


def _prep_w_in(w):
    g = A_HEADS // A_KV_HEADS
    aq = w[:, :A_Q]
    ak = jnp.repeat(w[:, A_Q:A_Q + A_KV].reshape(D_MODEL, A_KV_HEADS, 1, HEAD_DIM), g, axis=2).reshape(D_MODEL, A_Q)
    av = jnp.repeat(w[:, A_Q + A_KV:A_Q + 2 * A_KV].reshape(D_MODEL, A_KV_HEADS, 1, HEAD_DIM), g, axis=2).reshape(D_MODEL, A_Q)
    return jnp.concatenate([aq, ak, av, w[:, A_Q + 2 * A_KV:]], axis=1).astype(BF16)


_BAND_SUBBLOCKS = 16


def _banded_kernel(*refs, rr, tq, koff, wband, halo, has_sink, emit_lse):
    q_ref, kp_ref, kc_ref, kn_ref, vp_ref, vc_ref, vn_ref, bias_ref, hmask_ref = refs[:9]
    rest = refs[9:]
    if has_sink:
        sink_ref, rest = rest[0], rest[1:]
    o_ref = rest[0]
    lse_ref = rest[1] if emit_lse else None

    first = pl.program_id(2) == 0
    last = pl.program_id(2) == pl.num_programs(2) - 1
    col = lax.broadcasted_iota(jnp.int32, (1, wband), 1)
    lane_head = lax.broadcasted_iota(jnp.int32, (BLOCK, HEADS_W), 1) // HEAD_DIM
    if has_sink:
        sink_col = lax.broadcasted_iota(jnp.int32, (1, LANES), 1) == 0

        def in_col0(x, new):
            return jnp.concatenate([jnp.where(sink_col, new, x[:, :LANES]), x[:, LANES:]], axis=1)
    for r, i in itertools.product(range(rr), range(tq // BLOCK)):
        lo = BLOCK * i - koff
        hi = lo + wband

        def band(p_ref, c_ref, n_ref):
            parts = []
            if lo < 0:
                parts.append(p_ref[r, halo + lo:halo, :])
            parts.append(c_ref[r, max(lo, 0):min(hi, tq), :])
            if hi > tq:
                parts.append(n_ref[r, 0:hi - tq, :])
            return parts[0] if len(parts) == 1 else jnp.concatenate(parts, axis=0)

        kb = band(kp_ref, kc_ref, kn_ref)
        vb = band(vp_ref, vc_ref, vn_ref)
        q = q_ref[r, BLOCK * i:BLOCK * (i + 1), :]
        qs = jnp.concatenate([q] * 4, axis=0) * hmask_ref[...]
        s = _dot_nt(qs, kb) + bias_ref[...]
        if lo < 0:
            s = s + jnp.where(first & (col < -lo), NEG, 0.0)
        if hi > tq:
            s = s + jnp.where(last & (col >= wband - (hi - tq)), NEG, 0.0)
        if has_sink:
            s = in_col0(s, sink_ref[...])
        m = jnp.max(s, axis=-1, keepdims=True)
        p = jnp.exp(s - m)
        l = jnp.sum(p, axis=-1, keepdims=True)
        if has_sink:
            p = in_col0(p, 0.0)
        pv = _dot(p.astype(BF16), vb) / l
        o = jnp.zeros((BLOCK, HEADS_W), F32)
        lse_o = jnp.zeros((BLOCK, HEADS_W), F32)
        lse = m + jnp.log(l) if emit_lse else None
        for h in range(4):
            hr = slice(BLOCK * h, BLOCK * (h + 1))
            o = jnp.where(lane_head == h, pv[hr], o)
            if emit_lse:
                lse_o = jnp.where(lane_head == h, lse[hr], lse_o)
        o_ref[r, BLOCK * i:BLOCK * (i + 1), :] = o.astype(BF16)
        if emit_lse:
            lse_ref[r, BLOCK * i:BLOCK * (i + 1), :] = lse_o


def _band_tables(window, koff, wband, slopes_eff):
    il = np.arange(BLOCK)[:, None]
    c = np.arange(wband)[None, :]
    dist = np.abs(c - koff - il).astype(np.float64)
    bias = np.concatenate([np.where(dist <= window, -s * dist, NEG) for s in slopes_eff], axis=0)
    hmask = np.concatenate([np.broadcast_to((np.arange(HEADS_W) // HEAD_DIM == h)[None, :], (BLOCK, HEADS_W))
                            for h in range(4)], axis=0)
    return jnp.asarray(bias, dtype=F32), jnp.asarray(hmask, dtype=BF16)


def _banded_attention(q, k, v, *, window, slopes_eff, sink=None, emit_lse=False):
    B, d, Sd, _ = q.shape
    tq = min(_BAND_SUBBLOCKS * BLOCK, Sd)
    rr = min(d, max(1, _BAND_SUBBLOCKS * BLOCK // tq))
    koff, wband, halo = (BLOCK // 2, 2 * BLOCK, BLOCK) if window <= BLOCK // 2 else (3 * BLOCK // 2, 4 * BLOCK, 2 * BLOCK)
    halo = min(halo, tq)
    assert koff <= halo and wband - koff - BLOCK <= halo and window <= koff and window <= wband - koff - BLOCK
    nh, nhb = tq // halo, Sd // halo
    bias, hmask = _band_tables(window, koff, wband, slopes_eff)
    cur = pl.BlockSpec((None, rr, tq, HEADS_W), lambda b, r, i: (b, r, i, 0))
    prev = pl.BlockSpec((None, rr, halo, HEADS_W), lambda b, r, i: (b, r, jnp.maximum(i * nh - 1, 0), 0))
    nxt = pl.BlockSpec((None, rr, halo, HEADS_W), lambda b, r, i: (b, r, jnp.minimum((i + 1) * nh, nhb - 1), 0))
    const = lambda b, r, i: (0, 0)
    in_specs = [cur, prev, cur, nxt, prev, cur, nxt,
                pl.BlockSpec((4 * BLOCK, wband), const), pl.BlockSpec((4 * BLOCK, HEADS_W), const)]
    args = [q, k, k, k, v, v, v, bias, hmask]
    if sink is not None:
        assert koff > window
        in_specs.append(pl.BlockSpec((4 * BLOCK, LANES), const))
        args.append(jnp.broadcast_to(jnp.repeat(sink.astype(F32), BLOCK)[:, None], (4 * BLOCK, LANES)))
    out_shape = [jax.ShapeDtypeStruct((B, d, Sd, HEADS_W), BF16)]
    out_specs = [cur]
    if emit_lse:
        out_shape.append(jax.ShapeDtypeStruct((B, d, Sd, HEADS_W), F32))
        out_specs.append(cur)
    return pl.pallas_call(
        functools.partial(_banded_kernel, rr=rr, tq=tq, koff=koff, wband=wband, halo=halo, has_sink=sink is not None,
                          emit_lse=emit_lse),
        grid=(B, d // rr, Sd // tq),
        in_specs=in_specs,
        out_specs=out_specs,
        out_shape=out_shape,
        compiler_params=pltpu.CompilerParams(dimension_semantics=("parallel", "parallel", "parallel"),
                                             vmem_limit_bytes=VMEM_LIMIT_BYTES),
    )(*args)


_SOFTMAX_ROWS = 128
_PIPE_SLOTS = 4
_DIFF_ITEMS = 32


def _diff_kernel(coef_ref, qtab_ref, lam_ref, subln_ref, q_ref, k_ref, v_ref, o_ref,
                 qs_ref, m_ref, acc_ref, *bufs, tq, nk, nqt, lam_init):
    s_refs, p_refs, a_refs = bufs[0:4], bufs[4:8], bufs[8:12]
    lane = lax.broadcasted_iota(jnp.int32, (tq, LANES), 1)
    row = lax.broadcasted_iota(jnp.int32, (tq, LANES), 0)

    def q_tile(w):
        return pl.program_id(2) * nqt + w

    for w in range(nqt):
        q = q_ref[w * tq:(w + 1) * tq, :].astype(F32)
        pos = q_tile(w) * tq + row
        base = jnp.where(lane < _AUG_SPLITS, -((pos >> _BLOCK_LOG2) << _BLOCK_LOG2).astype(F32),
                         jnp.where(lane < 2 * _AUG_SPLITS, -(pos & (BLOCK - 1)).astype(F32), qtab_ref[...]))
        q1 = jnp.where(lane < HEAD_DIM, q, 0.0).astype(BF16)
        q2 = jnp.where(lane >= HEAD_DIM, q, 0.0).astype(BF16)
        for var, sign in enumerate((1.0, -1.0, 0.0)):
            aug = (sign * base).astype(BF16)
            qs_ref[w, var, 0:tq, 0:LANES] = q1
            qs_ref[w, var, 0:tq, LANES:2 * LANES] = aug
            qs_ref[w, var, tq:2 * tq, 0:LANES] = q2
            qs_ref[w, var, tq:2 * tq, LANES:2 * LANES] = aug
    m_ref[...] = jnp.full(m_ref.shape, NEG, F32)
    acc_ref[...] = jnp.zeros(acc_ref.shape, F32)

    def key_tile(n):
        w, t = divmod(n, nk)
        qi = q_tile(w)
        return qi if t == 0 else jnp.where(t <= qi, t - 1, t)

    def rows_of(j):
        return pl.ds(pl.multiple_of(j * tq, tq), tq)

    def stage1(n):
        w, t = divmod(n, nk)
        qi, j = q_tile(w), key_tile(n)
        if t == 0:
            s = _dot(qs_ref[w, 2], k_ref[:, rows_of(j)])
            r = lax.broadcasted_iota(jnp.int32, s.shape, 0) & (tq - 1)
            c = lax.broadcasted_iota(jnp.int32, s.shape, 1)
            s = s - coef_ref[:, 0:1] * jnp.abs(r - c).astype(F32)
        else:
            s = _dot(qs_ref[w, jnp.where(j < qi, 0, 1)], k_ref[:, rows_of(j)])
        s_refs[n % _PIPE_SLOTS][...] = s

    def stage2(n):
        w, slot = n // nk, n % _PIPE_SLOTS
        s_a, s_b = s_refs[slot], s_refs[slot + 1]
        for r0 in range(0, 2 * tq, _SOFTMAX_ROWS):
            rows = slice(r0, r0 + _SOFTMAX_ROWS)
            m_prev = m_ref[w, rows, :]
            row_max = jnp.max(jnp.maximum(s_a[rows, :], s_b[rows, :]), axis=-1, keepdims=True)
            m_new = jnp.maximum(m_prev, row_max)
            a_refs[slot][rows, :] = jnp.exp2(m_prev - m_new)
            m_ref[w, rows, :] = m_new
        for r0 in range(0, 2 * tq, _SOFTMAX_ROWS):
            rows = slice(r0, r0 + _SOFTMAX_ROWS)
            m_new = jnp.concatenate([m_ref[w, rows, :]] * (tq // LANES), axis=1)
            p_refs[slot][rows, :] = jnp.exp2(s_a[rows, :] - m_new).astype(BF16)
            p_refs[slot + 1][rows, :] = jnp.exp2(s_b[rows, :] - m_new).astype(BF16)

    def stage3(n):
        w, slot = n // nk, n % _PIPE_SLOTS
        pv = (_dot(p_refs[slot][...], v_ref[rows_of(key_tile(n)), :])
              + _dot(p_refs[slot + 1][...], v_ref[rows_of(key_tile(n + 1)), :]))
        a = a_refs[slot][...]
        acc_ref[w] = acc_ref[w] * jnp.concatenate([a, a], axis=1) + pv

    def finish(w):
        acc = acc_ref[w]
        o1 = acc[:tq, :LANES] / acc[:tq, LANES:]
        o2 = acc[tq:, :LANES] / acc[tq:, LANES:]
        lv = lam_ref[...]
        lam = (jnp.exp(jnp.sum(lv[0:1] * lv[1:2], axis=-1, keepdims=True))
               - jnp.exp(jnp.sum(lv[2:3] * lv[3:4], axis=-1, keepdims=True)) + lam_init)
        o = _rms(o1 - lam * o2, subln_ref[...], SUBLN_EPS) * (1.0 - lam_init)
        o_ref[w * tq:(w + 1) * tq, :] = o.astype(BF16)

    items = nqt * nk
    for u in range(items // 2 + 2):
        if 0 <= 2 * u - 2 < items:
            stage2(2 * u - 2)
        if 2 * u < items:
            stage1(2 * u)
            stage1(2 * u + 1)
        if 0 <= 2 * u - 4 < items:
            stage3(2 * u - 4)
            if (2 * u - 4) % nk == nk - 2:
                finish((2 * u - 4) // nk)


def _diff_attention(q, k, v, lam_vecs, subln, *, B, S, lam_init):
    tq = min(512, S // 4)
    nk = S // tq
    assert nk % 4 == 0
    nqt = max(1, min(nk, _DIFF_ITEMS // nk))
    _, qtab, coef = _diff_alibi_tables()
    qv = q.reshape(B, S, B_QK)
    vv = v.reshape(B_HEADS, B, S, 2 * LANES)
    out = pl.pallas_call(
        functools.partial(_diff_kernel, tq=tq, nk=nk, nqt=nqt, lam_init=lam_init),
        grid=(B, B_HEADS, nk // nqt),
        in_specs=[pl.BlockSpec((None, 1, LANES), lambda b, h, i: (h, 0, 0)),
                  pl.BlockSpec((None, 1, LANES), lambda b, h, i: (h, 0, 0)),
                  pl.BlockSpec((4, HEAD_DIM), lambda b, h, i: (0, 0)),
                  pl.BlockSpec((1, B_VDIM), lambda b, h, i: (0, 0)),
                  pl.BlockSpec((None, nqt * tq, LANES), lambda b, h, i: (b, i, h)),
                  pl.BlockSpec((None, None, 2 * LANES, S), lambda b, h, i: (h, b, 0, 0)),
                  pl.BlockSpec((None, None, S, 2 * LANES), lambda b, h, i: (h, b, 0, 0))],
        out_specs=pl.BlockSpec((None, nqt * tq, LANES), lambda b, h, i: (b, i, h)),
        out_shape=jax.ShapeDtypeStruct((B, S, B_V), BF16),
        scratch_shapes=[pltpu.VMEM((nqt, 3, 2 * tq, 2 * LANES), BF16),
                        pltpu.VMEM((nqt, 2 * tq, LANES), F32),
                        pltpu.VMEM((nqt, 2 * tq, 2 * LANES), F32),
                        *[pltpu.VMEM((2 * tq, tq), F32)] * _PIPE_SLOTS,
                        *[pltpu.VMEM((2 * tq, tq), BF16)] * _PIPE_SLOTS,
                        *[pltpu.VMEM((2 * tq, LANES), F32)] * _PIPE_SLOTS],
        compiler_params=pltpu.CompilerParams(
            dimension_semantics=("parallel", "parallel", "arbitrary"),
            vmem_limit_bytes=VMEM_LIMIT_BYTES),
    )(coef, qtab, lam_vecs.astype(F32), subln.reshape(1, B_VDIM).astype(F32), qv, k, vv)
    return out.reshape(B * S, B_V)


def _mixout_kernel(x_ref, oa_ref, ob_ref, g0_ref, g1_ref, g2_ref, l0_ref, l1_ref, l2_ref, w_ref, ln_ref,
                   xo_ref, h_ref, *slabs, tm):
    slabs = list(slabs)

    def tokens(ref, d):
        if d == 1:
            return ref[0].astype(F32)
        halves = []
        for c in range(HEADS_W // LANES):
            buf = slabs.pop()
            for r in range(d):
                buf[pl.ds(r, tm // d, stride=d), :] = ref[r, :, c * LANES:(c + 1) * LANES].astype(F32)
            halves.append(buf[...])
        return jnp.concatenate(halves, axis=1)

    dil = [d for _, d in C_PAIRS]
    l0, l1, l2 = tokens(l0_ref, dil[0]), tokens(l1_ref, dil[1]), tokens(l2_ref, dil[2])
    g0, g1, g2 = tokens(g0_ref, dil[0]), tokens(g1_ref, dil[1]), tokens(g2_ref, dil[2])
    mx = jnp.maximum(jnp.maximum(l0, l1), l2)
    e0, e1, e2 = jnp.exp(l0 - mx), jnp.exp(l1 - mx), jnp.exp(l2 - mx)
    oc = (e0 * g0 + e1 * g1 + e2 * g2) / (e0 + e1 + e2)
    mix = (_dot(oa_ref[...], w_ref[0:A_Q, :])
           + _dot(ob_ref[...], w_ref[A_Q:A_Q + B_V, :])
           + _dot(oc.astype(BF16), w_ref[A_Q + B_V:, :]))
    x = x_ref[...] + mix
    xo_ref[...] = x
    h_ref[...] = _rms(x, ln_ref[...], EPS).astype(BF16)


def _mix_out(x, oa, ob, og, lse, w_out, ln2, B, S):
    T = x.shape[0]
    tm = min(_ROW_TILE, S)
    nt = S // tm
    row = lambda i: (i, 0)
    const = lambda i: (0, 0)
    grp = [pl.BlockSpec((None, d, tm // d, HEADS_W), lambda i: (i // nt, 0, i % nt, 0)) for _, d in C_PAIRS]
    n_slabs = 2 * sum(1 for _, d in C_PAIRS if d > 1)
    return pl.pallas_call(
        functools.partial(_mixout_kernel, tm=tm),
        grid=(T // tm,),
        in_specs=[pl.BlockSpec((tm, D_MODEL), row),
                  pl.BlockSpec((tm, A_Q), row),
                  pl.BlockSpec((tm, B_V), row)]
                 + grp + grp
                 + [pl.BlockSpec((D_MODEL, D_MODEL), const),
                    pl.BlockSpec((1, D_MODEL), const)],
        out_specs=[pl.BlockSpec((tm, D_MODEL), row), pl.BlockSpec((tm, D_MODEL), row)],
        out_shape=[jax.ShapeDtypeStruct((T, D_MODEL), F32), jax.ShapeDtypeStruct((T, D_MODEL), BF16)],
        scratch_shapes=[pltpu.VMEM((tm, LANES), F32)] * (n_slabs * (HEADS_W // LANES)),
        compiler_params=pltpu.CompilerParams(dimension_semantics=("parallel",),
                                             vmem_limit_bytes=VMEM_LIMIT_BYTES),
    )(x, oa, ob, *og, *lse, w_out, ln2.reshape(1, D_MODEL))


_UP_CHUNK = 256
_UP_HALO = 16


def _upconv_kernel(hp_ref, hc_ref, hn_ref, w_ref, cw_ref, cb_ref, o_ref, hh_ref, *, tm, seq_len):
    pos0 = (pl.program_id(0) * tm) % seq_len
    keep_prev = jnp.where(pos0 == 0, 0.0, 1.0)
    keep_next = jnp.where(pos0 + tm == seq_len, 0.0, 1.0)
    ext = tm + 2 * _UP_HALO
    hh_ref[0:_UP_HALO, :] = (hp_ref[...].astype(F32) * keep_prev).astype(BF16)
    hh_ref[_UP_HALO:_UP_HALO + tm, :] = hc_ref[...]
    hh_ref[_UP_HALO + tm:ext, :] = (hn_ref[...].astype(F32) * keep_next).astype(BF16)
    hh = hh_ref[...]

    def conv(c0):
        cols = slice(c0, c0 + _UP_CHUNK)
        u = _dot(hh, w_ref[:, cols])
        return (pltpu.roll(u, 1, axis=0) * cw_ref[0:1, cols] + u * cw_ref[1:2, cols]
                + pltpu.roll(u, ext - 1, axis=0) * cw_ref[2:3, cols] + cb_ref[:, cols])

    for c0 in range(0, 2 * D_FF, _UP_CHUNK):
        o_ref[:, c0:c0 + _UP_CHUNK] = conv(c0)[_UP_HALO:_UP_HALO + tm, :].astype(BF16)


def _up_conv(h, w_up, conv_w, conv_b, *, seq_len):
    T = h.shape[0]
    tm = min(_ROW_TILE, T)
    nh = tm // _UP_HALO
    n_halo = T // _UP_HALO
    row = lambda i: (i, 0)
    const = lambda i: (0, 0)
    return pl.pallas_call(
        functools.partial(_upconv_kernel, tm=tm, seq_len=seq_len),
        grid=(T // tm,),
        in_specs=[pl.BlockSpec((_UP_HALO, D_MODEL), lambda i: (jnp.maximum(i * nh - 1, 0), 0)),
                  pl.BlockSpec((tm, D_MODEL), row),
                  pl.BlockSpec((_UP_HALO, D_MODEL), lambda i: (jnp.minimum((i + 1) * nh, n_halo - 1), 0)),
                  pl.BlockSpec((D_MODEL, 2 * D_FF), const, pipeline_mode=_RESIDENT),
                  pl.BlockSpec((3, 2 * D_FF), const),
                  pl.BlockSpec((1, 2 * D_FF), const)],
        out_specs=pl.BlockSpec((tm, 2 * D_FF), row),
        out_shape=jax.ShapeDtypeStruct((T, 2 * D_FF), BF16),
        scratch_shapes=[pltpu.VMEM((tm + 2 * _UP_HALO, D_MODEL), BF16)],
        compiler_params=pltpu.CompilerParams(dimension_semantics=("parallel",),
                                             vmem_limit_bytes=VMEM_LIMIT_BYTES),
    )(h, h, h, w_up, conv_w, conv_b.reshape(1, 2 * D_FF))


def _down_kernel(c_ref, w_ref, x_ref, lnf_ref, o_ref, act_ref, *, final_norm):
    for c0 in range(0, D_FF, _UP_CHUNK):
        half = 0.5 * c_ref[:, c0:c0 + _UP_CHUNK].astype(F32)
        val = c_ref[:, D_FF + c0:D_FF + c0 + _UP_CHUNK].astype(F32)
        act_ref[:, c0:c0 + _UP_CHUNK] = ((half + half * jnp.tanh(half)) * val).astype(BF16)
    x = x_ref[...] + _dot(act_ref[...], w_ref[...])
    if final_norm:
        x = _rms(x, lnf_ref[...], EPS)
    o_ref[...] = x


def _down(c, w_down, x, ln_f, *, final_norm):
    T = x.shape[0]
    tm = min(_ROW_TILE, T)
    row = lambda i: (i, 0)
    const = lambda i: (0, 0)
    return pl.pallas_call(
        functools.partial(_down_kernel, final_norm=final_norm),
        grid=(T // tm,),
        in_specs=[pl.BlockSpec((tm, 2 * D_FF), row),
                  pl.BlockSpec((D_FF, D_MODEL), const, pipeline_mode=_RESIDENT),
                  pl.BlockSpec((tm, D_MODEL), row),
                  pl.BlockSpec((1, D_MODEL), const)],
        out_specs=pl.BlockSpec((tm, D_MODEL), row),
        out_shape=jax.ShapeDtypeStruct((T, D_MODEL), F32),
        scratch_shapes=[pltpu.VMEM((tm, D_FF), BF16)],
        compiler_params=pltpu.CompilerParams(dimension_semantics=("parallel",),
                                             vmem_limit_bytes=VMEM_LIMIT_BYTES),
    )(c, w_down, x, ln_f.reshape(1, D_MODEL))


def _trunk(x, p):
    B, S = x.shape[0], x.shape[1]
    xf = x.reshape(B * S, D_MODEL)
    a_slopes = _alibi_slopes(A_HEADS)
    c_slopes = _alibi_slopes(C_GROUPS * C_HEADS)
    for l in range(DEPTH):
        aq, ak, av, bq, bk, bv, *c = _in_proj(xf, p["ln1"][l], p["w_in"][l], B, S)
        as_seq = lambda t: t.reshape(B, 1, S, HEADS_W)
        (oa,) = _banded_attention(as_seq(aq), as_seq(ak), as_seq(av), window=A_WINDOW, slopes_eff=a_slopes,
                                  sink=p["a_sink"][l])
        lam_init = 0.8 - 0.6 * math.exp(-0.3 * l)
        ob = _diff_attention(bq, bk, bv, p["lam"][l], p["subln"][l], B=B, S=S, lam_init=lam_init)
        og, lse = [], []
        for g, (w, d) in enumerate(C_PAIRS):
            o_g, lse_g = _banded_attention(
                c[g], c[3 + g], c[6 + g], window=w // (2 * d),
                slopes_eff=[s * d for s in c_slopes[g * C_HEADS:(g + 1) * C_HEADS]], emit_lse=True)
            og.append(o_g)
            lse.append(lse_g)
        xf, h2 = _mix_out(xf, oa.reshape(B * S, HEADS_W), ob, og, lse, p["w_out"][l], p["ln2"][l], B, S)
        c_mlp = _up_conv(h2, p["w_up"][l], p["conv_w"][l], p["conv_b"][l], seq_len=S)
        xf = _down(c_mlp, p["w_down"][l], xf, p["ln_f"], final_norm=(l == DEPTH - 1))
    return xf.reshape(B, S, D_MODEL)


def kernel(x_prompt, x_sample, ln1, w_in, a_sink, lam_q1, lam_k1, lam_q2, lam_k2, subln, w_out, ln2, w_up,
           conv_w, conv_b, w_down, ln_f):
    p = {
        "ln1": ln1,
        "w_in": [_prep_w_in(w_in[l]) for l in range(DEPTH)],
        "a_sink": a_sink,
        "lam": jnp.stack([lam_q1, lam_k1, lam_q2, lam_k2], axis=1),
        "subln": subln,
        "w_out": w_out.astype(BF16),
        "ln2": ln2,
        "w_up": w_up.astype(BF16),
        "conv_w": conv_w,
        "conv_b": conv_b,
        "w_down": w_down.astype(BF16),
        "ln_f": ln_f,
    }
    return (_trunk(x_prompt, p), _trunk(x_sample, p))
```

```python
import functools
import itertools
import math

import numpy as np
import jax
import jax.numpy as jnp
from jax import lax
from jax.experimental import pallas as pl
from jax.experimental.pallas import tpu as pltpu

F32 = jnp.float32
BF16 = jnp.bfloat16

D_MODEL = 1024
DEPTH = 2
HEAD_DIM = 64
BLOCK = 128
EPS = 1e-6
SUBLN_EPS = 1e-5
NEG = -1e30
SCALE = HEAD_DIM ** -0.5
A_HEADS = 4
A_KV_HEADS = 2
A_WINDOW = 128
B_HEADS = 4
B_VDIM = 2 * HEAD_DIM
C_PAIRS = ((128, 1), (512, 4), (2048, 16))
C_GROUPS = 3
C_HEADS = 4
D_FF = 2816
A_Q = A_HEADS * HEAD_DIM
A_KV = A_KV_HEADS * HEAD_DIM
B_QK = B_HEADS * 2 * HEAD_DIM
B_V = B_HEADS * B_VDIM
C_QKV = C_GROUPS * C_HEADS * HEAD_DIM
HEADS_W = 4 * HEAD_DIM

VMEM_LIMIT_BYTES = 56 * 1024 * 1024
LANES = 128
SUBLANES = 8
_ROW_TILE = 1024
_RESIDENT = pl.Buffered(1)


def _alibi_slopes(n):
    return [2.0 ** (-8.0 * k / n) for k in range(1, n + 1)]


LOG2E = math.log2(math.e)
_AUG_SPLITS = 3
_BLOCK_LOG2 = BLOCK.bit_length() - 1


def _bf16_round(x):
    b = np.asarray(x, np.float32).view(np.uint32)
    return ((b + (((b >> 16) & 1) + 0x7FFF)) & 0xFFFF0000).astype(np.uint32).view(np.float32)


def _diff_alibi_tables():
    c = np.asarray(_alibi_slopes(B_HEADS), np.float64) * LOG2E
    pieces, rem = [], c.copy()
    for _ in range(_AUG_SPLITS):
        pk = _bf16_round(rem).astype(np.float64)
        pieces.append(pk)
        rem = rem - pk
    pieces = np.stack(pieces, axis=1)
    ktab = np.zeros((B_HEADS, LANES), np.float32)
    qtab = np.zeros((B_HEADS, 1, LANES), np.float32)
    n = _AUG_SPLITS
    ktab[:, 0:n] = pieces
    ktab[:, n:2 * n] = pieces
    qtab[:, 0, 2 * n:3 * n] = pieces
    qtab[:, 0, 3 * n:4 * n] = pieces
    cfull = np.broadcast_to(c.astype(np.float32)[:, None, None], (B_HEADS, 1, LANES)).copy()
    return jnp.asarray(ktab), jnp.asarray(qtab), jnp.asarray(cfull)


def _dot(a, b):
    return jnp.dot(a, b, preferred_element_type=F32)


def _dot_nt(a, b):
    return lax.dot_general(a, b, (((1,), (1,)), ((), ())), preferred_element_type=F32)


def _rms(x, g, eps):
    return x * lax.rsqrt(jnp.mean(x * x, axis=-1, keepdims=True) + eps) * g


_W_AQ, _W_AK, _W_AV = 0, 256, 512
_W_BQ, _W_BK, _W_BV = 768, 1280, 1792
_W_CQ, _W_CK, _W_CV = 2304, 3072, 3840
_W_COLS = 4608


def _inproj_kernel(x_ref, g_ref, w_ref, ktab_ref, aq_ref, ak_ref, av_ref, bq_ref, bk_ref, bv_ref, *rest, seq_len, tm):
    c_refs, slabs = rest[:9], rest[9:]
    h = _rms(x_ref[...], g_ref[...], EPS).astype(BF16)

    def proj(c0, c1):
        return _dot(h, w_ref[:, c0:c1])

    aq_ref[...] = (proj(_W_AQ, _W_AK) * (SCALE * LOG2E)).astype(BF16)
    ak_ref[...] = proj(_W_AK, _W_AV).astype(BF16)
    av_ref[...] = proj(_W_AV, _W_BQ).astype(BF16)
    bq_ref[...] = (proj(_W_BQ, _W_BK) * (SCALE * LOG2E)).astype(BF16)
    pos = (pl.program_id(0) * tm + lax.broadcasted_iota(jnp.int32, (tm, LANES), 0)) % seq_len
    lane = lax.broadcasted_iota(jnp.int32, (tm, LANES), 1)
    pos_cols = jnp.where(lane < 3 * _AUG_SPLITS, ((pos >> _BLOCK_LOG2) << _BLOCK_LOG2).astype(F32),
                         jnp.where(lane < 4 * _AUG_SPLITS, (pos & (BLOCK - 1)).astype(F32), 0.0))
    ones = jnp.ones((tm, LANES), BF16)
    bk = proj(_W_BK, _W_BV)
    bv = proj(_W_BV, _W_CQ).astype(BF16)
    for hh in range(B_HEADS):
        k_aug = jnp.concatenate([bk[:, hh * LANES:(hh + 1) * LANES],
                                 jnp.where(lane < 2 * _AUG_SPLITS, ktab_ref[hh:hh + 1, :], pos_cols)], axis=1)
        bk_ref[hh] = k_aug.T.astype(BF16)
        bv_ref[hh, :, 0:LANES] = bv[:, hh * LANES:(hh + 1) * LANES]
        bv_ref[hh, :, LANES:2 * LANES] = ones
    slab = 0
    for part, (c0, scale) in enumerate(((_W_CQ, SCALE * LOG2E), (_W_CK, 1.0), (_W_CV, 1.0))):
        pc = proj(c0, c0 + C_QKV) * scale
        for g, (_, d) in enumerate(C_PAIRS):
            dst = c_refs[3 * part + g]
            val = pc[:, g * HEADS_W:(g + 1) * HEADS_W]
            if d == 1:
                dst[0] = val.astype(BF16)
            else:
                for c in range(HEADS_W // LANES):
                    buf = slabs[slab]
                    slab += 1
                    lanes = slice(c * LANES, (c + 1) * LANES)
                    buf[...] = val[:, lanes]
                    for r in range(d):
                        dst[r, :, lanes] = buf[pl.ds(r, tm // d, stride=d), :].astype(BF16)


def _in_proj(x, g, w, B, S):
    T = x.shape[0]
    tm = min(_ROW_TILE, S)
    nt = S // tm
    row = lambda i: (i, 0)
    const = lambda i: (0, 0)
    outs = [
        (jax.ShapeDtypeStruct((T, 256), BF16), pl.BlockSpec((tm, 256), row)),
        (jax.ShapeDtypeStruct((T, 256), BF16), pl.BlockSpec((tm, 256), row)),
        (jax.ShapeDtypeStruct((T, 256), BF16), pl.BlockSpec((tm, 256), row)),
        (jax.ShapeDtypeStruct((T, B_QK), BF16), pl.BlockSpec((tm, B_QK), row)),
        (jax.ShapeDtypeStruct((B_HEADS, B, 256, S), BF16),
         pl.BlockSpec((B_HEADS, None, 256, tm), lambda i: (0, i // nt, 0, i % nt))),
        (jax.ShapeDtypeStruct((B_HEADS, T, 256), BF16), pl.BlockSpec((B_HEADS, tm, 256), lambda i: (0, i, 0))),
    ]
    for _ in range(3):
        for _, d in C_PAIRS:
            outs.append((jax.ShapeDtypeStruct((B, d, S // d, HEADS_W), BF16),
                         pl.BlockSpec((None, d, tm // d, HEADS_W), lambda i: (i // nt, 0, i % nt, 0))))
    n_slabs = 3 * sum(1 for _, d in C_PAIRS if d > 1)
    return pl.pallas_call(
        functools.partial(_inproj_kernel, seq_len=S, tm=tm),
        grid=(T // tm,),
        in_specs=[pl.BlockSpec((tm, D_MODEL), row),
                  pl.BlockSpec((1, D_MODEL), const),
                  pl.BlockSpec((D_MODEL, _W_COLS), const, pipeline_mode=_RESIDENT),
                  pl.BlockSpec((B_HEADS, LANES), const)],
        out_specs=[o[1] for o in outs],
        out_shape=[o[0] for o in outs],
        scratch_shapes=[pltpu.VMEM((tm, LANES), F32)] * (n_slabs * (HEADS_W // LANES)),
        compiler_params=pltpu.CompilerParams(dimension_semantics=("parallel",),
                                             vmem_limit_bytes=VMEM_LIMIT_BYTES,
                                             allow_input_fusion=[False, False, True, False]),
    )(x, g.reshape(1, D_MODEL), w, _diff_alibi_tables()[0])


def _prep_w_in(w):
    g = A_HEADS // A_KV_HEADS
    aq = w[:, :A_Q]
    ak = jnp.repeat(w[:, A_Q:A_Q + A_KV].reshape(D_MODEL, A_KV_HEADS, 1, HEAD_DIM), g, axis=2).reshape(D_MODEL, A_Q)
    av = jnp.repeat(w[:, A_Q + A_KV:A_Q + 2 * A_KV].reshape(D_MODEL, A_KV_HEADS, 1, HEAD_DIM), g, axis=2).reshape(D_MODEL, A_Q)
    return jnp.concatenate([aq, ak, av, w[:, A_Q + 2 * A_KV:]], axis=1).astype(BF16)


_BAND_SUBBLOCKS = 16


def _banded_kernel(*refs, rr, tq, koff, wband, halo, has_sink, emit_lse):
    q_ref, kp_ref, kc_ref, kn_ref, vp_ref, vc_ref, vn_ref, bias_ref, hmask_ref = refs[:9]
    rest = refs[9:]
    if has_sink:
        sink_ref, rest = rest[0], rest[1:]
    o_ref = rest[0]
    lse_ref = rest[1] if emit_lse else None

    first = pl.program_id(2) == 0
    last = pl.program_id(2) == pl.num_programs(2) - 1
    col = lax.broadcasted_iota(jnp.int32, (1, wband), 1)
    lane_head = lax.broadcasted_iota(jnp.int32, (BLOCK, HEADS_W), 1) // HEAD_DIM
    if has_sink:
        sink_col = lax.broadcasted_iota(jnp.int32, (1, LANES), 1) == 0

        def in_col0(x, new):
            return jnp.concatenate([jnp.where(sink_col, new, x[:, :LANES]), x[:, LANES:]], axis=1)
    for r, i in itertools.product(range(rr), range(tq // BLOCK)):
        lo = BLOCK * i - koff
        hi = lo + wband

        def band(p_ref, c_ref, n_ref):
            parts = []
            if lo < 0:
                parts.append(p_ref[r, halo + lo:halo, :])
            parts.append(c_ref[r, max(lo, 0):min(hi, tq), :])
            if hi > tq:
                parts.append(n_ref[r, 0:hi - tq, :])
            return parts[0] if len(parts) == 1 else jnp.concatenate(parts, axis=0)

        kb = band(kp_ref, kc_ref, kn_ref)
        vb = band(vp_ref, vc_ref, vn_ref)
        q = q_ref[r, BLOCK * i:BLOCK * (i + 1), :]
        qs = jnp.concatenate([q] * 4, axis=0) * hmask_ref[...]
        s = _dot_nt(qs, kb) + bias_ref[...]
        if lo < 0:
            s = s + jnp.where(first & (col < -lo), NEG, 0.0)
        if hi > tq:
            s = s + jnp.where(last & (col >= wband - (hi - tq)), NEG, 0.0)
        if has_sink:
            s = in_col0(s, sink_ref[...])
        m = jnp.max(s, axis=-1, keepdims=True)
        p = jnp.exp2(s - m)
        l = jnp.sum(p, axis=-1, keepdims=True)
        if has_sink:
            p = in_col0(p, 0.0)
        pv = _dot(p.astype(BF16), vb) / l
        o = jnp.zeros((BLOCK, HEADS_W), F32)
        lse_o = jnp.zeros((BLOCK, HEADS_W), F32)
        lse = m + jnp.log2(l) if emit_lse else None
        for h in range(4):
            hr = slice(BLOCK * h, BLOCK * (h + 1))
            o = jnp.where(lane_head == h, pv[hr], o)
            if emit_lse:
                lse_o = jnp.where(lane_head == h, lse[hr], lse_o)
        o_ref[r, BLOCK * i:BLOCK * (i + 1), :] = o.astype(BF16)
        if emit_lse:
            lse_ref[r, BLOCK * i:BLOCK * (i + 1), :] = lse_o


def _band_tables(window, koff, wband, slopes_eff):
    il = np.arange(BLOCK)[:, None]
    c = np.arange(wband)[None, :]
    dist = np.abs(c - koff - il).astype(np.float64)
    bias = np.concatenate([np.where(dist <= window, -s * LOG2E * dist, NEG) for s in slopes_eff], axis=0)
    hmask = np.concatenate([np.broadcast_to((np.arange(HEADS_W) // HEAD_DIM == h)[None, :], (BLOCK, HEADS_W))
                            for h in range(4)], axis=0)
    return jnp.asarray(bias, dtype=F32), jnp.asarray(hmask, dtype=BF16)


def _banded_attention(q, k, v, *, window, slopes_eff, sink=None, emit_lse=False):
    B, d, Sd, _ = q.shape
    tq = min(_BAND_SUBBLOCKS * BLOCK, Sd)
    rr = min(d, max(1, _BAND_SUBBLOCKS * BLOCK // tq))
    koff, wband, halo = (BLOCK // 2, 2 * BLOCK, BLOCK) if window <= BLOCK // 2 else (3 * BLOCK // 2, 4 * BLOCK, 2 * BLOCK)
    halo = min(halo, tq)
    assert koff <= halo and wband - koff - BLOCK <= halo and window <= koff and window <= wband - koff - BLOCK
    nh, nhb = tq // halo, Sd // halo
    bias, hmask = _band_tables(window, koff, wband, slopes_eff)
    cur = pl.BlockSpec((None, rr, tq, HEADS_W), lambda b, r, i: (b, r, i, 0))
    prev = pl.BlockSpec((None, rr, halo, HEADS_W), lambda b, r, i: (b, r, jnp.maximum(i * nh - 1, 0), 0))
    nxt = pl.BlockSpec((None, rr, halo, HEADS_W), lambda b, r, i: (b, r, jnp.minimum((i + 1) * nh, nhb - 1), 0))
    const = lambda b, r, i: (0, 0)
    in_specs = [cur, prev, cur, nxt, prev, cur, nxt,
                pl.BlockSpec((4 * BLOCK, wband), const), pl.BlockSpec((4 * BLOCK, HEADS_W), const)]
    args = [q, k, k, k, v, v, v, bias, hmask]
    if sink is not None:
        assert koff > window
        in_specs.append(pl.BlockSpec((4 * BLOCK, LANES), const))
        args.append(jnp.broadcast_to(jnp.repeat(sink.astype(F32) * LOG2E, BLOCK)[:, None], (4 * BLOCK, LANES)))
    out_shape = [jax.ShapeDtypeStruct((B, d, Sd, HEADS_W), BF16)]
    out_specs = [cur]
    if emit_lse:
        out_shape.append(jax.ShapeDtypeStruct((B, d, Sd, HEADS_W), F32))
        out_specs.append(cur)
    return pl.pallas_call(
        functools.partial(_banded_kernel, rr=rr, tq=tq, koff=koff, wband=wband, halo=halo, has_sink=sink is not None,
                          emit_lse=emit_lse),
        grid=(B, d // rr, Sd // tq),
        in_specs=in_specs,
        out_specs=out_specs,
        out_shape=out_shape,
        compiler_params=pltpu.CompilerParams(dimension_semantics=("parallel", "parallel", "parallel"),
                                             vmem_limit_bytes=VMEM_LIMIT_BYTES),
    )(*args)


_SOFTMAX_ROWS = 128
_PIPE_SLOTS = 4
_DIFF_ITEMS = 32


def _diff_kernel(diag_ref, qtab_ref, lam_ref, subln_ref, q_ref, k_ref, v_ref, o_ref,
                 qs_ref, m_ref, acc_ref, *bufs, tq, nk, nqt, lam_init):
    s_refs, p_refs, a_refs = bufs[0:4], bufs[4:8], bufs[8:10]
    lane = lax.broadcasted_iota(jnp.int32, (tq, LANES), 1)
    row = lax.broadcasted_iota(jnp.int32, (tq, LANES), 0)

    def q_tile(w):
        return pl.program_id(2) * nqt + w

    for w in range(nqt):
        q = q_ref[w * tq:(w + 1) * tq, :].astype(F32)
        pos = q_tile(w) * tq + row
        base = jnp.where(lane < _AUG_SPLITS, -((pos >> _BLOCK_LOG2) << _BLOCK_LOG2).astype(F32),
                         jnp.where(lane < 2 * _AUG_SPLITS, -(pos & (BLOCK - 1)).astype(F32), qtab_ref[...]))
        q1 = jnp.where(lane < HEAD_DIM, q, 0.0).astype(BF16)
        q2 = jnp.where(lane >= HEAD_DIM, q, 0.0).astype(BF16)
        for var, sign in enumerate((1.0, -1.0, 0.0)):
            aug = (sign * base).astype(BF16)
            qs_ref[w, var, 0:tq, 0:LANES] = q1
            qs_ref[w, var, 0:tq, LANES:2 * LANES] = aug
            qs_ref[w, var, tq:2 * tq, 0:LANES] = q2
            qs_ref[w, var, tq:2 * tq, LANES:2 * LANES] = aug
    m_ref[...] = jnp.full(m_ref.shape, NEG, F32)
    acc_ref[...] = jnp.zeros(acc_ref.shape, F32)

    def key_tile(n):
        w, t = divmod(n, nk)
        qi = q_tile(w)
        return qi if t == 0 else jnp.where(t <= qi, t - 1, t)

    def rows_of(j):
        return pl.ds(pl.multiple_of(j * tq, tq), tq)

    def stage1(n):
        w, t = divmod(n, nk)
        qi, j = q_tile(w), key_tile(n)
        if t == 0:
            bias = diag_ref[...]
            s = _dot(qs_ref[w, 2], k_ref[:, rows_of(j)]) + jnp.concatenate([bias, bias], axis=0)
        else:
            s = _dot(qs_ref[w, jnp.where(j < qi, 0, 1)], k_ref[:, rows_of(j)])
        s_refs[n % _PIPE_SLOTS][...] = s

    def stage2(n):
        w, slot = n // nk, n % _PIPE_SLOTS
        s_a, s_b = s_refs[slot], s_refs[slot + 1]
        for r0 in range(0, 2 * tq, _SOFTMAX_ROWS):
            rows = slice(r0, r0 + _SOFTMAX_ROWS)
            m_prev = m_ref[w, rows, :]
            row_max = jnp.max(jnp.maximum(s_a[rows, :], s_b[rows, :]), axis=-1, keepdims=True)
            m_new = jnp.maximum(m_prev, row_max)
            a_refs[slot // 2][rows, :] = jnp.exp2(m_prev - m_new)
            m_ref[w, rows, :] = m_new
        for r0 in range(0, 2 * tq, _SOFTMAX_ROWS):
            rows = slice(r0, r0 + _SOFTMAX_ROWS)
            m_new = jnp.concatenate([m_ref[w, rows, :]] * (tq // LANES), axis=1)
            p_refs[slot][rows, :] = jnp.exp2(s_a[rows, :] - m_new).astype(BF16)
            p_refs[slot + 1][rows, :] = jnp.exp2(s_b[rows, :] - m_new).astype(BF16)

    def stage3(n):
        w, slot = n // nk, n % _PIPE_SLOTS
        pv = (_dot(p_refs[slot][...], v_ref[rows_of(key_tile(n)), :])
              + _dot(p_refs[slot + 1][...], v_ref[rows_of(key_tile(n + 1)), :]))
        a = a_refs[slot // 2][...]
        acc_ref[w] = acc_ref[w] * jnp.concatenate([a, a], axis=1) + pv

    def finish(w):
        acc = acc_ref[w]
        o1 = acc[:tq, :LANES] / acc[:tq, LANES:]
        o2 = acc[tq:, :LANES] / acc[tq:, LANES:]
        lv = lam_ref[...]
        lam = (jnp.exp(jnp.sum(lv[0:1] * lv[1:2], axis=-1, keepdims=True))
               - jnp.exp(jnp.sum(lv[2:3] * lv[3:4], axis=-1, keepdims=True)) + lam_init)
        o = _rms(o1 - lam * o2, subln_ref[...], SUBLN_EPS) * (1.0 - lam_init)
        o_ref[w * tq:(w + 1) * tq, :] = o.astype(BF16)

    items = nqt * nk
    for u in range(items // 2 + 2):
        if 0 <= 2 * u - 2 < items:
            stage2(2 * u - 2)
        if 2 * u < items:
            stage1(2 * u)
            stage1(2 * u + 1)
        if 0 <= 2 * u - 4 < items:
            stage3(2 * u - 4)
            if (2 * u - 4) % nk == nk - 2:
                finish((2 * u - 4) // nk)


def _diff_attention(q, k, v, lam_vecs, subln, *, B, S, lam_init):
    tq = min(512, S // 4)
    nk = S // tq
    assert nk % 4 == 0
    nqt = max(1, min(nk, _DIFF_ITEMS // nk))
    _, qtab, coef = _diff_alibi_tables()
    in_tile = jnp.arange(tq, dtype=jnp.int32)
    diag_bias = -coef[:, :, :1] * jnp.abs(in_tile[:, None] - in_tile[None, :]).astype(F32)
    qv = q.reshape(B, S, B_QK)
    vv = v.reshape(B_HEADS, B, S, 2 * LANES)
    out = pl.pallas_call(
        functools.partial(_diff_kernel, tq=tq, nk=nk, nqt=nqt, lam_init=lam_init),
        grid=(B, B_HEADS, nk // nqt),
        in_specs=[pl.BlockSpec((None, tq, tq), lambda b, h, i: (h, 0, 0), pipeline_mode=_RESIDENT),
                  pl.BlockSpec((None, 1, LANES), lambda b, h, i: (h, 0, 0)),
                  pl.BlockSpec((4, HEAD_DIM), lambda b, h, i: (0, 0)),
                  pl.BlockSpec((1, B_VDIM), lambda b, h, i: (0, 0)),
                  pl.BlockSpec((None, nqt * tq, LANES), lambda b, h, i: (b, i, h)),
                  pl.BlockSpec((None, None, 2 * LANES, S), lambda b, h, i: (h, b, 0, 0)),
                  pl.BlockSpec((None, None, S, 2 * LANES), lambda b, h, i: (h, b, 0, 0))],
        out_specs=pl.BlockSpec((None, nqt * tq, LANES), lambda b, h, i: (b, i, h)),
        out_shape=jax.ShapeDtypeStruct((B, S, B_V), BF16),
        scratch_shapes=[pltpu.VMEM((nqt, 3, 2 * tq, 2 * LANES), BF16),
                        pltpu.VMEM((nqt, 2 * tq, LANES), F32),
                        pltpu.VMEM((nqt, 2 * tq, 2 * LANES), F32),
                        *[pltpu.VMEM((2 * tq, tq), F32)] * _PIPE_SLOTS,
                        *[pltpu.VMEM((2 * tq, tq), BF16)] * _PIPE_SLOTS,
                        *[pltpu.VMEM((2 * tq, LANES), F32)] * (_PIPE_SLOTS // 2)],
        compiler_params=pltpu.CompilerParams(
            dimension_semantics=("parallel", "parallel", "arbitrary"),
            vmem_limit_bytes=VMEM_LIMIT_BYTES),
    )(diag_bias, qtab, lam_vecs.astype(F32), subln.reshape(1, B_VDIM).astype(F32), qv, k, vv)
    return out.reshape(B * S, B_V)


def _mixout_kernel(x_ref, oa_ref, ob_ref, g0_ref, g1_ref, g2_ref, l0_ref, l1_ref, l2_ref, w_ref, ln_ref,
                   xo_ref, h_ref, *slabs, tm):
    slabs = list(slabs)

    def tokens(ref, d):
        if d == 1:
            return ref[0].astype(F32)
        halves = []
        for c in range(HEADS_W // LANES):
            buf = slabs.pop()
            for r in range(d):
                buf[pl.ds(r, tm // d, stride=d), :] = ref[r, :, c * LANES:(c + 1) * LANES].astype(F32)
            halves.append(buf[...])
        return jnp.concatenate(halves, axis=1)

    dil = [d for _, d in C_PAIRS]
    l0, l1, l2 = tokens(l0_ref, dil[0]), tokens(l1_ref, dil[1]), tokens(l2_ref, dil[2])
    g0, g1, g2 = tokens(g0_ref, dil[0]), tokens(g1_ref, dil[1]), tokens(g2_ref, dil[2])
    mx = jnp.maximum(jnp.maximum(l0, l1), l2)
    e0, e1, e2 = jnp.exp2(l0 - mx), jnp.exp2(l1 - mx), jnp.exp2(l2 - mx)
    oc = (e0 * g0 + e1 * g1 + e2 * g2) / (e0 + e1 + e2)
    mix = (_dot(oa_ref[...], w_ref[0:A_Q, :])
           + _dot(ob_ref[...], w_ref[A_Q:A_Q + B_V, :])
           + _dot(oc.astype(BF16), w_ref[A_Q + B_V:, :]))
    x = x_ref[...] + mix
    xo_ref[...] = x
    h_ref[...] = _rms(x, ln_ref[...], EPS).astype(BF16)


def _mix_out(x, oa, ob, og, lse, w_out, ln2, B, S):
    T = x.shape[0]
    tm = min(_ROW_TILE, S)
    nt = S // tm
    row = lambda i: (i, 0)
    const = lambda i: (0, 0)
    grp = [pl.BlockSpec((None, d, tm // d, HEADS_W), lambda i: (i // nt, 0, i % nt, 0)) for _, d in C_PAIRS]
    n_slabs = 2 * sum(1 for _, d in C_PAIRS if d > 1)
    return pl.pallas_call(
        functools.partial(_mixout_kernel, tm=tm),
        grid=(T // tm,),
        in_specs=[pl.BlockSpec((tm, D_MODEL), row),
                  pl.BlockSpec((tm, A_Q), row),
                  pl.BlockSpec((tm, B_V), row)]
                 + grp + grp
                 + [pl.BlockSpec((D_MODEL, D_MODEL), const),
                    pl.BlockSpec((1, D_MODEL), const)],
        out_specs=[pl.BlockSpec((tm, D_MODEL), row), pl.BlockSpec((tm, D_MODEL), row)],
        out_shape=[jax.ShapeDtypeStruct((T, D_MODEL), F32), jax.ShapeDtypeStruct((T, D_MODEL), BF16)],
        scratch_shapes=[pltpu.VMEM((tm, LANES), F32)] * (n_slabs * (HEADS_W // LANES)),
        compiler_params=pltpu.CompilerParams(dimension_semantics=("parallel",),
                                             vmem_limit_bytes=VMEM_LIMIT_BYTES,
                                             allow_input_fusion=[False] * 9 + [True, False]),
    )(x, oa, ob, *og, *lse, w_out, ln2.reshape(1, D_MODEL))


_UP_CHUNK = 256
_UP_HALO = 16


def _upconv_kernel(hp_ref, hc_ref, hn_ref, w_ref, cw_ref, o_ref, hh_ref, *, tm, seq_len):
    pos0 = (pl.program_id(0) * tm) % seq_len
    keep_prev = jnp.where(pos0 == 0, 0.0, 1.0)
    keep_next = jnp.where(pos0 + tm == seq_len, 0.0, 1.0)
    ext = tm + 2 * _UP_HALO
    hh_ref[0:_UP_HALO, :] = (hp_ref[...].astype(F32) * keep_prev).astype(BF16)
    hh_ref[_UP_HALO:_UP_HALO + tm, :] = hc_ref[...]
    hh_ref[_UP_HALO + tm:ext, :] = (hn_ref[...].astype(F32) * keep_next).astype(BF16)
    hh = hh_ref[...]

    def conv(c0):
        cols = slice(c0, c0 + _UP_CHUNK)
        u = _dot(hh, w_ref[:, cols])
        return (pltpu.roll(u, 1, axis=0) * cw_ref[0:1, cols] + u * cw_ref[1:2, cols]
                + pltpu.roll(u, ext - 1, axis=0) * cw_ref[2:3, cols])

    for c0 in range(0, 2 * D_FF, _UP_CHUNK):
        o_ref[:, c0:c0 + _UP_CHUNK] = conv(c0)[_UP_HALO:_UP_HALO + tm, :].astype(BF16)


def _up_conv(h, w_up, conv_w, *, seq_len):
    T = h.shape[0]
    tm = min(_ROW_TILE, T)
    nh = tm // _UP_HALO
    n_halo = T // _UP_HALO
    row = lambda i: (i, 0)
    const = lambda i: (0, 0)
    return pl.pallas_call(
        functools.partial(_upconv_kernel, tm=tm, seq_len=seq_len),
        grid=(T // tm,),
        in_specs=[pl.BlockSpec((_UP_HALO, D_MODEL), lambda i: (jnp.maximum(i * nh - 1, 0), 0)),
                  pl.BlockSpec((tm, D_MODEL), row),
                  pl.BlockSpec((_UP_HALO, D_MODEL), lambda i: (jnp.minimum((i + 1) * nh, n_halo - 1), 0)),
                  pl.BlockSpec((D_MODEL, 2 * D_FF), const, pipeline_mode=_RESIDENT),
                  pl.BlockSpec((3, 2 * D_FF), const)],
        out_specs=pl.BlockSpec((tm, 2 * D_FF), row),
        out_shape=jax.ShapeDtypeStruct((T, 2 * D_FF), BF16),
        scratch_shapes=[pltpu.VMEM((tm + 2 * _UP_HALO, D_MODEL), BF16)],
        compiler_params=pltpu.CompilerParams(dimension_semantics=("parallel",),
                                             vmem_limit_bytes=VMEM_LIMIT_BYTES,
                                             allow_input_fusion=[False, False, False, True, False]),
    )(h, h, h, w_up, conv_w)


def _down_kernel(c_ref, cb_ref, w_ref, x_ref, lnf_ref, o_ref, act_ref, *, final_norm):
    for c0 in range(0, D_FF, _UP_CHUNK):
        gate, vals = slice(c0, c0 + _UP_CHUNK), slice(D_FF + c0, D_FF + c0 + _UP_CHUNK)
        half = 0.5 * (c_ref[:, gate].astype(F32) + cb_ref[:, gate])
        val = c_ref[:, vals].astype(F32) + cb_ref[:, vals]
        act_ref[:, c0:c0 + _UP_CHUNK] = ((half + half * jnp.tanh(half)) * val).astype(BF16)
    x = x_ref[...] + _dot(act_ref[...], w_ref[...])
    if final_norm:
        x = _rms(x, lnf_ref[...], EPS)
    o_ref[...] = x


def _down(c, conv_b, w_down, x, ln_f, *, final_norm):
    T = x.shape[0]
    tm = min(_ROW_TILE, T)
    row = lambda i: (i, 0)
    const = lambda i: (0, 0)
    return pl.pallas_call(
        functools.partial(_down_kernel, final_norm=final_norm),
        grid=(T // tm,),
        in_specs=[pl.BlockSpec((tm, 2 * D_FF), row),
                  pl.BlockSpec((1, 2 * D_FF), const),
                  pl.BlockSpec((D_FF, D_MODEL), const, pipeline_mode=_RESIDENT),
                  pl.BlockSpec((tm, D_MODEL), row),
                  pl.BlockSpec((1, D_MODEL), const)],
        out_specs=pl.BlockSpec((tm, D_MODEL), row),
        out_shape=jax.ShapeDtypeStruct((T, D_MODEL), F32),
        scratch_shapes=[pltpu.VMEM((tm, D_FF), BF16)],
        compiler_params=pltpu.CompilerParams(dimension_semantics=("parallel",),
                                             vmem_limit_bytes=VMEM_LIMIT_BYTES,
                                             allow_input_fusion=[False, False, True, False, False]),
    )(c, conv_b.reshape(1, 2 * D_FF), w_down, x, ln_f.reshape(1, D_MODEL))


def _trunk(x, p):
    B, S = x.shape[0], x.shape[1]
    xf = x.reshape(B * S, D_MODEL)
    a_slopes = _alibi_slopes(A_HEADS)
    c_slopes = _alibi_slopes(C_GROUPS * C_HEADS)
    for l in range(DEPTH):
        aq, ak, av, bq, bk, bv, *c = _in_proj(xf, p["ln1"][l], p["w_in"][l], B, S)
        as_seq = lambda t: t.reshape(B, 1, S, HEADS_W)
        (oa,) = _banded_attention(as_seq(aq), as_seq(ak), as_seq(av), window=A_WINDOW, slopes_eff=a_slopes,
                                  sink=p["a_sink"][l])
        lam_init = 0.8 - 0.6 * math.exp(-0.3 * l)
        ob = _diff_attention(bq, bk, bv, p["lam"][l], p["subln"][l], B=B, S=S, lam_init=lam_init)
        og, lse = [], []
        for g, (w, d) in enumerate(C_PAIRS):
            o_g, lse_g = _banded_attention(
                c[g], c[3 + g], c[6 + g], window=w // (2 * d),
                slopes_eff=[s * d for s in c_slopes[g * C_HEADS:(g + 1) * C_HEADS]], emit_lse=True)
            og.append(o_g)
            lse.append(lse_g)
        xf, h2 = _mix_out(xf, oa.reshape(B * S, HEADS_W), ob, og, lse, p["w_out"][l].astype(BF16), p["ln2"][l], B, S)
        c_mlp = _up_conv(h2, p["w_up"][l].astype(BF16), p["conv_w"][l], seq_len=S)
        xf = _down(c_mlp, p["conv_b"][l], p["w_down"][l].astype(BF16), xf, p["ln_f"], final_norm=(l == DEPTH - 1))
    return xf.reshape(B, S, D_MODEL)


def kernel(x_prompt, x_sample, ln1, w_in, a_sink, lam_q1, lam_k1, lam_q2, lam_k2, subln, w_out, ln2, w_up,
           conv_w, conv_b, w_down, ln_f):
    p = {
        "ln1": ln1,
        "w_in": [_prep_w_in(w_in[l]) for l in range(DEPTH)],
        "a_sink": a_sink,
        "lam": jnp.stack([lam_q1, lam_k1, lam_q2, lam_k2], axis=1),
        "subln": subln,
        "w_out": w_out,
        "ln2": ln2,
        "w_up": w_up,
        "conv_w": conv_w,
        "conv_b": conv_b,
        "w_down": w_down,
        "ln_f": ln_f,
    }
    return (_trunk(x_prompt, p), _trunk(x_sample, p))
```
